```python
import math
import jax, jax.numpy as jnp
from jax import lax
import numpy as np

D_MODEL = 1024
BATCH = 32
SEQ = 2048
DEPTH = 2

HEAD_DIM = 64
BLOCK = 128
LRU_WIDTH = D_MODEL // 2
LRU_BLOCKS = 8
LRU_BLOCK = LRU_WIDTH // LRU_BLOCKS
CONV_WIDTH = 4
LRU_C = 8.0
FOX_HEADS = (D_MODEL // 2) // HEAD_DIM
DIL_HEADS = (D_MODEL // 2) // HEAD_DIM
DIL_PATTERN = ((128, 1), (512, 4), (2048, 16))
SWA_Q_HEADS = (D_MODEL // 2) // HEAD_DIM
SWA_KV_HEADS = 2
SWA_WINDOW = 128
MEM_LEN = 256
XA_HEADS = 4
XA_HEAD_DIM = D_MODEL // XA_HEADS
D_FF = 4 * D_MODEL
ROPE_THETA = 10000.0
EPS = 1e-6

N_EVEN = (DEPTH + 1) // 2
N_ODD = DEPTH // 2
FOX_W = FOX_HEADS * HEAD_DIM
DIL_W = DIL_HEADS * HEAD_DIM
SWA_QW = SWA_Q_HEADS * HEAD_DIM
SWA_KW = SWA_KV_HEADS * HEAD_DIM
AB_IN = 2 * LRU_WIDTH + 3 * FOX_W + FOX_HEADS
AB_MIX = LRU_WIDTH + FOX_W
CD_IN = 3 * DIL_W + SWA_QW + 2 * SWA_KW
CD_MIX = DIL_W + SWA_QW

kernel_name = "hybrid_rglru_fox_dilated_swa_trunk"


def rmsnorm(x, g):
    xf = x.astype(jnp.float32)
    y = xf * lax.rsqrt(jnp.mean(xf * xf, axis=-1, keepdims=True) + EPS)
    return (y * g.astype(jnp.float32)).astype(x.dtype)


def rope(x, pos):
    half = x.shape[-1] // 2
    inv = ROPE_THETA ** (-jnp.arange(half, dtype=jnp.float32) / half)
    ang = pos.astype(jnp.float32)[:, None] * inv[None, :]
    cos = jnp.cos(ang)[None, :, None, :]
    sin = jnp.sin(ang)[None, :, None, :]
    xf = x.astype(jnp.float32)
    x1, x2 = xf[..., :half], xf[..., half:]
    return jnp.concatenate([x1 * cos - x2 * sin, x2 * cos + x1 * sin], axis=-1).astype(x.dtype)


def causal_depthwise_conv(u, w, b):
    out = lax.conv_general_dilated(
        u, w[:, None, :].astype(u.dtype), window_strides=(1,),
        padding=[(CONV_WIDTH - 1, 0)], dimension_numbers=("NWC", "WIO", "NWC"),
        feature_group_count=u.shape[-1])
    return out + b.astype(u.dtype)


def _linear_recurrence_combine(e1, e2):
    a1, b1 = e1
    a2, b2 = e2
    return a1 * a2, a2 * b1 + b2


def rg_lru(u, w_a, b_a, w_i, b_i, lam):
    bsz, s, c = u.shape
    uf = u.astype(jnp.float32)
    ub = uf.reshape(bsz, s, LRU_BLOCKS, LRU_BLOCK)
    r = jax.nn.sigmoid(jnp.einsum("bsgc,gcd->bsgd", ub, w_a.astype(jnp.float32)).reshape(bsz, s, c) + b_a.astype(jnp.float32))
    i = jax.nn.sigmoid(jnp.einsum("bsgc,gcd->bsgd", ub, w_i.astype(jnp.float32)).reshape(bsz, s, c) + b_i.astype(jnp.float32))
    log_a = -LRU_C * r * jax.nn.softplus(-lam.astype(jnp.float32))
    a = jnp.exp(log_a)
    x_in = jnp.sqrt(-jnp.expm1(2.0 * log_a)) * (i * uf)
    _, h = lax.associative_scan(_linear_recurrence_combine, (a, x_in), axis=1)
    return h.astype(u.dtype)


def forgetting_attention(q, k, v, log_f):
    s_len = q.shape[1]
    scale = q.shape[-1] ** -0.5
    c = jnp.cumsum(log_f, axis=1).transpose(0, 2, 1)
    outs = []
    for blk in range(s_len // BLOCK):
        q0, q1 = blk * BLOCK, (blk + 1) * BLOCK
        sc = jnp.einsum("bqhd,bkhd->bhqk", q[:, q0:q1], k[:, :q1]).astype(jnp.float32) * scale
        bias = c[:, :, q0:q1, None] - c[:, :, None, :q1]
        mask = jnp.arange(q0, q1)[:, None] >= jnp.arange(q1)[None, :]
        p = jax.nn.softmax(jnp.where(mask, sc + bias, -jnp.inf), axis=-1)
        outs.append(jnp.einsum("bhqk,bkhd->bqhd", p, v[:, :q1].astype(jnp.float32)))
    return jnp.concatenate(outs, axis=1).astype(q.dtype)


def banded_partial(q, k, v, max_dist):
    n, l, hk, g, dh = q.shape
    nb = -(-l // BLOCK)
    lp = nb * BLOCK
    pad = lp - l
    qb = jnp.pad(q, ((0, 0), (0, pad), (0, 0), (0, 0), (0, 0))).reshape(n, nb, BLOCK, hk, g, dh)
    kp = jnp.pad(k, ((0, 0), (BLOCK, pad), (0, 0), (0, 0))).reshape(n, nb + 1, BLOCK, hk, dh)
    vp = jnp.pad(v, ((0, 0), (BLOCK, pad), (0, 0), (0, 0))).reshape(n, nb + 1, BLOCK, hk, dh)
    kb = jnp.concatenate([kp[:, :-1], kp[:, 1:]], axis=2)
    vb = jnp.concatenate([vp[:, :-1], vp[:, 1:]], axis=2)
    sc = jnp.einsum("nbqhgd,nbkhd->nbhgqk", qb, kb).astype(jnp.float32) * dh ** -0.5
    qpos = jnp.arange(BLOCK)[:, None] + BLOCK
    kpos = jnp.arange(2 * BLOCK)[None, :]
    dist = qpos - kpos
    blk = jnp.arange(nb)[:, None, None]
    valid = (dist >= 0) & (dist <= max_dist) & (blk * BLOCK - BLOCK + kpos >= 0)
    sc = jnp.where(valid[None, :, None, None], sc, -jnp.inf)
    m = jnp.max(sc, axis=-1)
    p = jnp.exp(sc - m[..., None])
    den = jnp.sum(p, axis=-1)
    num = jnp.einsum("nbhgqk,nbkhd->nbqhgd", p, vb.astype(jnp.float32)).reshape(n, lp, hk, g, dh)[:, :l]
    m = m.transpose(0, 1, 4, 2, 3).reshape(n, lp, hk, g)[:, :l]
    den = den.transpose(0, 1, 4, 2, 3).reshape(n, lp, hk, g)[:, :l]
    return num, m, den


def dilated_attention(q, k, v):
    bsz, s, h, dh = q.shape
    nums, ms, dens = [], [], []
    for window, dil in DIL_PATTERN:
        l = s // dil

        def to_classes(t):
            return t.reshape(bsz, l, dil, h, dh).transpose(0, 2, 1, 3, 4).reshape(bsz * dil, l, h, dh)

        num, m, den = banded_partial(to_classes(q)[:, :, :, None, :], to_classes(k), to_classes(v), window // dil)
        nums.append(num[:, :, :, 0].reshape(bsz, dil, l, h, dh).transpose(0, 2, 1, 3, 4).reshape(bsz, s, h, dh))
        ms.append(m[..., 0].reshape(bsz, dil, l, h).transpose(0, 2, 1, 3).reshape(bsz, s, h))
        dens.append(den[..., 0].reshape(bsz, dil, l, h).transpose(0, 2, 1, 3).reshape(bsz, s, h))
    m_tot = ms[0]
    for m in ms[1:]:
        m_tot = jnp.maximum(m_tot, m)
    num_tot = sum(nm * jnp.exp(m - m_tot)[..., None] for nm, m in zip(nums, ms))
    den_tot = sum(dn * jnp.exp(m - m_tot) for dn, m in zip(dens, ms))
    return (num_tot / den_tot[..., None]).astype(q.dtype)


def sliding_window_sink_attention(q, k, v, sink):
    bsz, s, hq, dh = q.shape
    hk = k.shape[2]
    g = hq // hk
    num, m, den = banded_partial(q.reshape(bsz, s, hk, g, dh), k, v, SWA_WINDOW - 1)
    sink_g = sink.astype(jnp.float32).reshape(hk, g)
    m2 = jnp.maximum(m, sink_g)
    w = jnp.exp(m - m2)
    out = num * w[..., None] / (den * w + jnp.exp(sink_g - m2))[..., None]
    return out.reshape(bsz, s, hq, dh).astype(q.dtype)


def mixer_rglru_fox(h, w_in, conv_w, conv_b, w_a, b_a, w_i, b_i, lam, b_f, w_out):
    bsz, s, _ = h.shape
    z = h @ w_in
    u, gate, q, k, v, f_logit = jnp.split(
        z, [LRU_WIDTH, 2 * LRU_WIDTH, 2 * LRU_WIDTH + FOX_W, 2 * LRU_WIDTH + 2 * FOX_W, 2 * LRU_WIDTH + 3 * FOX_W], axis=-1)
    y_a = rg_lru(causal_depthwise_conv(u, conv_w, conv_b), w_a, b_a, w_i, b_i, lam) * jax.nn.gelu(gate)
    log_f = jax.nn.log_sigmoid(f_logit.astype(jnp.float32) + b_f.astype(jnp.float32))
    shp = (bsz, s, FOX_HEADS, HEAD_DIM)
    y_b = forgetting_attention(q.reshape(shp), k.reshape(shp), v.reshape(shp), log_f).reshape(bsz, s, FOX_W)
    return jnp.concatenate([y_a, y_b], axis=-1) @ w_out


def mixer_dilated_swa(h, w_in, sink, w_out):
    bsz, s, _ = h.shape
    pos = jnp.arange(s)
    z = h @ w_in
    qc, kc, vc, qd, kd, vd = jnp.split(
        z, [DIL_W, 2 * DIL_W, 3 * DIL_W, 3 * DIL_W + SWA_QW, 3 * DIL_W + SWA_QW + SWA_KW], axis=-1)
    dshp = (bsz, s, DIL_HEADS, HEAD_DIM)
    y_c = dilated_attention(rope(qc.reshape(dshp), pos), rope(kc.reshape(dshp), pos), vc.reshape(dshp))
    kshp = (bsz, s, SWA_KV_HEADS, HEAD_DIM)
    y_d = sliding_window_sink_attention(rope(qd.reshape(bsz, s, SWA_Q_HEADS, HEAD_DIM), pos),
                                        rope(kd.reshape(kshp), pos), vd.reshape(kshp), sink)
    return jnp.concatenate([y_c.reshape(bsz, s, DIL_W), y_d.reshape(bsz, s, SWA_QW)], axis=-1) @ w_out


def memory_cross_attention(h, mem, g_mem, w_q, w_kv, w_o):
    bsz, s, _ = h.shape
    mem_n = rmsnorm(mem, g_mem)
    q = (h @ w_q).reshape(bsz, s, XA_HEADS, XA_HEAD_DIM)
    k, v = jnp.split(mem_n @ w_kv, 2, axis=-1)
    k = k.reshape(bsz, -1, XA_HEADS, XA_HEAD_DIM)
    v = v.reshape(bsz, -1, XA_HEADS, XA_HEAD_DIM)
    sc = jnp.einsum("bqhd,bkhd->bhqk", q, k).astype(jnp.float32) * XA_HEAD_DIM ** -0.5
    p = jax.nn.softmax(sc, axis=-1)
    o = jnp.einsum("bhqk,bkhd->bqhd", p, v.astype(jnp.float32)).astype(h.dtype)
    return o.reshape(bsz, s, D_MODEL) @ w_o


def squared_relu_mlp(h, w_up, w_down):
    return jnp.square(jax.nn.relu(h @ w_up)) @ w_down


def setup_inputs(seed: int = 0) -> dict:
    key = jax.random.key(seed)
    ks = iter(jax.random.split(key, 40))
    f32 = jnp.float32

    def nrm(shape, scale):
        return scale * jax.random.normal(next(ks), shape, f32)

    def gain(shape):
        return 1.0 + 0.05 * jax.random.normal(next(ks), shape, f32)

    a8 = jax.random.uniform(next(ks), (N_EVEN, LRU_WIDTH), f32, 0.9, 0.999)
    a0 = a8 ** (1.0 / LRU_C)
    lru_lambda = jnp.log(a0) - jnp.log1p(-a0)
    return {
        "x": nrm((BATCH, SEQ, D_MODEL), 1.0),
        "mem": nrm((BATCH, MEM_LEN, D_MODEL), 1.0),
        "ab_norm": gain((N_EVEN, D_MODEL)),
        "ab_w_in": nrm((N_EVEN, D_MODEL, AB_IN), D_MODEL ** -0.5),
        "ab_conv_w": nrm((N_EVEN, CONV_WIDTH, LRU_WIDTH), CONV_WIDTH ** -0.5),
        "ab_conv_b": nrm((N_EVEN, LRU_WIDTH), 0.05),
        "lru_w_a": nrm((N_EVEN, LRU_BLOCKS, LRU_BLOCK, LRU_BLOCK), LRU_BLOCK ** -0.5),
        "lru_b_a": nrm((N_EVEN, LRU_WIDTH), 0.05),
        "lru_w_i": nrm((N_EVEN, LRU_BLOCKS, LRU_BLOCK, LRU_BLOCK), LRU_BLOCK ** -0.5),
        "lru_b_i": nrm((N_EVEN, LRU_WIDTH), 0.05),
        "lru_lambda": lru_lambda,
        "fox_b_f": jax.random.uniform(next(ks), (N_EVEN, FOX_HEADS), f32, 1.0, 4.0),
        "ab_w_out": nrm((N_EVEN, AB_MIX, D_MODEL), AB_MIX ** -0.5),
        "cd_norm": gain((N_ODD, D_MODEL)),
        "cd_w_in": nrm((N_ODD, D_MODEL, CD_IN), D_MODEL ** -0.5),
        "cd_sink": nrm((N_ODD, SWA_Q_HEADS), 0.5),
        "cd_w_out": nrm((N_ODD, CD_MIX, D_MODEL), CD_MIX ** -0.5),
        "xa_norm": gain((DEPTH, D_MODEL)),
        "xa_mem_norm": gain((DEPTH, D_MODEL)),
        "xa_w_q": nrm((DEPTH, D_MODEL, D_MODEL), D_MODEL ** -0.5),
        "xa_w_kv": nrm((DEPTH, D_MODEL, 2 * D_MODEL), D_MODEL ** -0.5),
        "xa_w_o": nrm((DEPTH, D_MODEL, D_MODEL), D_MODEL ** -0.5),
        "mlp_norm": gain((DEPTH, D_MODEL)),
        "mlp_w_up": nrm((DEPTH, D_MODEL, D_FF), D_MODEL ** -0.5),
        "mlp_w_down": nrm((DEPTH, D_FF, D_MODEL), D_FF ** -0.5),
        "final_norm": gain((D_MODEL,)),
    }


def reference(x, mem, ab_norm, ab_w_in, ab_conv_w, ab_conv_b, lru_w_a, lru_b_a, lru_w_i, lru_b_i,
              lru_lambda, fox_b_f, ab_w_out, cd_norm, cd_w_in, cd_sink, cd_w_out,
              xa_norm, xa_mem_norm, xa_w_q, xa_w_kv, xa_w_o, mlp_norm, mlp_w_up, mlp_w_down, final_norm):
    h = x
    for layer in range(DEPTH):
        j = layer // 2
        if layer % 2 == 0:
            h = h + mixer_rglru_fox(rmsnorm(h, ab_norm[j]), ab_w_in[j], ab_conv_w[j], ab_conv_b[j],
                                    lru_w_a[j], lru_b_a[j], lru_w_i[j], lru_b_i[j], lru_lambda[j],
                                    fox_b_f[j], ab_w_out[j])
        else:
            h = h + mixer_dilated_swa(rmsnorm(h, cd_norm[j]), cd_w_in[j], cd_sink[j], cd_w_out[j])
        h = h + memory_cross_attention(rmsnorm(h, xa_norm[layer]), mem, xa_mem_norm[layer],
                                       xa_w_q[layer], xa_w_kv[layer], xa_w_o[layer])
        h = h + squared_relu_mlp(rmsnorm(h, mlp_norm[layer]), mlp_w_up[layer], mlp_w_down[layer])
    return rmsnorm(h, final_norm)
```

```python
import functools
import math

import jax
import jax.numpy as jnp
from jax import lax
from jax.experimental import pallas as pl
from jax.experimental.pallas import tpu as pltpu

F32 = jnp.float32
BF16 = jnp.bfloat16

D_MODEL = 1024
HEAD_DIM = 64
N_HEADS = 8
BRANCH_W = N_HEADS * HEAD_DIM
LRU_WIDTH = 512
LRU_BLOCKS = 8
CONV_WIDTH = 4
LRU_C = 8.0
SWA_KV_HEADS = 2
SWA_KW = SWA_KV_HEADS * HEAD_DIM
SWA_WINDOW = 128
DIL_NEAR_WINDOW = 128
DIL_MID_WINDOW = 512
DIL_MID_STEP = 4
DIL_FAR_STEP = 16
MEM_LEN = 256
XA_HEADS = 4
XA_HEAD_DIM = D_MODEL // XA_HEADS
D_FF = 4 * D_MODEL
ROPE_THETA = 10000.0
EPS = 1e-6
ATTN_SCALE = HEAD_DIM ** -0.5
XA_SCALE = XA_HEAD_DIM ** -0.5
LANES = 128
NEG_INF = float("-inf")

ROW_TILE = 512
SCAN_TILE = 256
ATTN_TILE = 256
VMEM_LIMIT = 48 * 1024 * 1024


def _params(*sem):
    return pltpu.CompilerParams(dimension_semantics=sem, vmem_limit_bytes=VMEM_LIMIT)


def _resident(shape):
    zeros = (0,) * len(shape)
    return pl.BlockSpec(shape, lambda *_: zeros, pipeline_mode=pl.Buffered(1))


def _rows(tile, width):
    return pl.BlockSpec((tile, width), lambda i: (i, 0))


def _rmsnorm(x, g):
    return x * lax.rsqrt(jnp.mean(x * x, axis=-1, keepdims=True) + EPS) * g


def _mm(a, b):
    return jnp.dot(a, b, preferred_element_type=F32)


def _mm_nt(a, b):
    return lax.dot_general(a, b, (((1,), (1,)), ((), ())), preferred_element_type=F32)


def _inproj_ab_kernel(x_ref, g_ref, w_ref, wf_ref, u_ref, gate_ref, q_ref, k_ref, v_ref, f_ref):
    xn = _rmsnorm(x_ref[...], g_ref[...]).astype(BF16)
    w = BRANCH_W
    u_ref[...] = _mm(xn, w_ref[:, 0:w])
    gate_ref[...] = _mm(xn, w_ref[:, w:2 * w])
    q_ref[...] = (_mm(xn, w_ref[:, 2 * w:3 * w]) * ATTN_SCALE).astype(BF16)
    k_ref[...] = _mm(xn, w_ref[:, 3 * w:4 * w]).astype(BF16)
    v_ref[...] = _mm(xn, w_ref[:, 4 * w:5 * w]).astype(BF16)
    f_ref[...] = _mm(xn, wf_ref[...])


def _inproj_ab(h, g, w_main, w_f):
    t = h.shape[0]
    sds = jax.ShapeDtypeStruct
    return pl.pallas_call(
        _inproj_ab_kernel,
        grid=(t // ROW_TILE,),
        in_specs=[_rows(ROW_TILE, D_MODEL), _resident(g.shape), _resident(w_main.shape), _resident(w_f.shape)],
        out_specs=[_rows(ROW_TILE, BRANCH_W)] * 5 + [_rows(ROW_TILE, LANES)],
        out_shape=[sds((t, BRANCH_W), F32), sds((t, BRANCH_W), F32), sds((t, BRANCH_W), BF16),
                   sds((t, BRANCH_W), BF16), sds((t, BRANCH_W), BF16), sds((t, LANES), F32)],
        compiler_params=_params("parallel"),
        name="inproj_ab",
    )(h, g, w_main, w_f)


def _inproj_cd_kernel(x_ref, g_ref, w_ref, cos_ref, sin_ref, qc_ref, kc_ref, vc_ref, qd_ref, kd_ref, vd_ref):
    xn = _rmsnorm(x_ref[...], g_ref[...]).astype(BF16)
    cos = cos_ref[...]
    sin = sin_ref[...]
    lane = lax.broadcasted_iota(jnp.int32, cos.shape, 1)
    first_half = (lane % HEAD_DIM) < (HEAD_DIM // 2)

    def rope(z, scale):
        outs = []
        for c in range(z.shape[1] // LANES):
            zc = z[:, c * LANES:(c + 1) * LANES]
            partner = jnp.where(first_half, pltpu.roll(zc, LANES - HEAD_DIM // 2, 1), pltpu.roll(zc, HEAD_DIM // 2, 1))
            outs.append(((zc * cos + partner * sin) * scale).astype(BF16))
        return outs[0] if len(outs) == 1 else jnp.concatenate(outs, axis=1)

    w = BRANCH_W
    qc_ref[...] = rope(_mm(xn, w_ref[:, 0:w]), ATTN_SCALE)
    kc_ref[...] = rope(_mm(xn, w_ref[:, w:2 * w]), 1.0)
    vc_ref[...] = _mm(xn, w_ref[:, 2 * w:3 * w]).astype(BF16)
    qd_ref[...] = rope(_mm(xn, w_ref[:, 3 * w:4 * w]), ATTN_SCALE)
    kd_ref[...] = rope(_mm(xn, w_ref[:, 4 * w:4 * w + SWA_KW]), 1.0)
    vd_ref[...] = _mm(xn, w_ref[:, 4 * w + SWA_KW:4 * w + 2 * SWA_KW]).astype(BF16)


def _inproj_cd(h, g, w, cos, sin, seq):
    t = h.shape[0]
    sds = jax.ShapeDtypeStruct
    tiles_per_seq = seq // ROW_TILE
    table = pl.BlockSpec((ROW_TILE, LANES), lambda i: (i % tiles_per_seq, 0))
    return pl.pallas_call(
        _inproj_cd_kernel,
        grid=(t // ROW_TILE,),
        in_specs=[_rows(ROW_TILE, D_MODEL), _resident(g.shape), _resident(w.shape), table, table],
        out_specs=[_rows(ROW_TILE, BRANCH_W)] * 4 + [_rows(ROW_TILE, SWA_KW)] * 2,
        out_shape=[sds((t, BRANCH_W), BF16)] * 4 + [sds((t, SWA_KW), BF16)] * 2,
        compiler_params=_params("parallel"),
        name="inproj_cd",
    )(h, g, w, cos, sin)


def _scan_linear(a, x):
    n = a.shape[0]
    row = lax.broadcasted_iota(jnp.int32, a.shape, 0)
    d = 1
    while d < n:
        keep = row >= d
        x = jnp.where(keep, x + a * pltpu.roll(x, d, 0), x)
        a = jnp.where(keep, a * pltpu.roll(a, d, 0), a)
        d *= 2
    return a, x


def _scan_sum(x):
    n = x.shape[0]
    row = lax.broadcasted_iota(jnp.int32, x.shape, 0)
    d = 1
    while d < n:
        x = jnp.where(row >= d, x + pltpu.roll(x, d, 0), x)
        d *= 2
    return x


def _softplus(x):
    return jnp.maximum(x, 0.0) + jnp.log1p(jnp.exp(-jnp.abs(x)))


def _sigmoid(x):
    return 1.0 / (1.0 + jnp.exp(-x))


def _gelu_tanh(x):
    return 0.5 * x * (1.0 + jnp.tanh(math.sqrt(2.0 / math.pi) * (x + 0.044715 * (x * x * x))))


def _lru_kernel(u_ref, gate_ref, f_ref, cw_ref, cb_ref, wa_ref, ba_ref, wi_ref, bi_ref, lam_ref, bf_ref,
                ya_ref, c_ref, ubuf, hprev, cprev):
    tc = u_ref.shape[0]
    tail = CONV_WIDTH * 2

    @pl.when(pl.program_id(1) == 0)
    def _():
        ubuf[0:tail, :] = jnp.zeros((tail, LRU_WIDTH), F32)
        hprev[...] = jnp.zeros_like(hprev)
        cprev[...] = jnp.zeros_like(cprev)

    u = u_ref[...]
    ubuf[tail:tail + tc, :] = u
    cw = cw_ref[...]
    conv = cb_ref[...] + cw[CONV_WIDTH - 1:CONV_WIDTH, :] * u
    for back in range(1, CONV_WIDTH):
        k = CONV_WIDTH - 1 - back
        conv = conv + cw[k:k + 1, :] * ubuf[tail - back:tail - back + tc, :]
    ubuf[0:tail, :] = u[tc - tail:tc, :]

    conv_b = conv.astype(BF16)
    r = _sigmoid(_mm(conv_b, wa_ref[...]) + ba_ref[...])
    gate_i = _sigmoid(_mm(conv_b, wi_ref[...]) + bi_ref[...])
    log_a = (-LRU_C) * r * _softplus(-lam_ref[...])
    a = jnp.exp(log_a)
    x_in = jnp.sqrt(-jnp.tanh(log_a) * (a * a + 1.0)) * (gate_i * conv)
    a_cum, h_loc = _scan_linear(a, x_in)
    h = h_loc + a_cum * hprev[7:8, :]
    hprev[...] = h[tc - 8:tc, :]
    ya_ref[...] = (h * _gelu_tanh(gate_ref[...])).astype(BF16)

    z = f_ref[...] + bf_ref[...]
    log_f = jnp.minimum(z, 0.0) - jnp.log1p(jnp.exp(-jnp.abs(z)))
    c = _scan_sum(log_f) + cprev[7:8, :]
    cprev[...] = c[tc - 8:tc, :]
    c_ref[...] = c


def _lru(u, gate, f, conv_w, conv_b, wa, ba, wi, bi, lam, bf, batch, seq):
    t = u.shape[0]
    tc = SCAN_TILE
    nt = seq // tc
    sds = jax.ShapeDtypeStruct

    def rows(width):
        return pl.BlockSpec((tc, width), lambda b, j: (b * nt + j, 0))

    small = [conv_w, conv_b, wa, ba, wi, bi, lam, bf]
    return pl.pallas_call(
        _lru_kernel,
        grid=(batch, nt),
        in_specs=[rows(LRU_WIDTH), rows(LRU_WIDTH), rows(LANES)] + [_resident(a.shape) for a in small],
        out_specs=[rows(LRU_WIDTH), rows(LANES)],
        out_shape=[sds((t, LRU_WIDTH), BF16), sds((t, LANES), F32)],
        scratch_shapes=[pltpu.VMEM((tc + 2 * CONV_WIDTH, LRU_WIDTH), F32),
                        pltpu.VMEM((8, LRU_WIDTH), F32), pltpu.VMEM((8, LANES), F32)],
        compiler_params=_params("parallel", "arbitrary"),
        name="rglru",
    )(u, gate, f, *small)


def _head(ref, rows, h):
    return ref[rows, h * HEAD_DIM:(h + 1) * HEAD_DIM]


def _flash_step(carry, q, k, v, bias):
    m, l, acc = carry
    s = _mm_nt(q, k)
    if bias is not None:
        s = s + bias
    m_new = jnp.maximum(m, jnp.max(s, axis=-1, keepdims=True))
    alpha = jnp.exp(m - m_new)
    p = jnp.exp(s - m_new)
    l = alpha * l + jnp.sum(p, axis=-1, keepdims=True)
    acc = alpha * acc + _mm(p.astype(BF16), v)
    return m_new, l, acc


def _flash_init(tq):
    return (jnp.full((tq, 1), NEG_INF, F32), jnp.zeros((tq, 1), F32), jnp.zeros((tq, HEAD_DIM), F32))


def _fox_kernel(q_ref, k_ref, v_ref, c_ref, o_ref):
    tq = q_ref.shape[0]
    i = pl.program_id(1)
    row = lax.broadcasted_iota(jnp.int32, (tq, tq), 0)
    col = lax.broadcasted_iota(jnp.int32, (tq, tq), 1)
    causal = jnp.where(row >= col, 0.0, NEG_INF)
    outs = []
    for h in range(N_HEADS):
        q = _head(q_ref, slice(None), h)

        def block(j, carry, extra=None):
            rows = pl.ds(pl.multiple_of(j * tq, tq), tq)
            bias = -c_ref[h:h + 1, rows]
            if extra is not None:
                bias = bias + extra
            return _flash_step(carry, q, _head(k_ref, rows, h), _head(v_ref, rows, h), bias)

        carry = lax.fori_loop(0, i, block, _flash_init(tq))
        _, l, acc = block(i, carry, causal)
        outs.append(acc / l)
    o_ref[...] = jnp.concatenate(outs, axis=1).astype(BF16)


def _fox(q, k, v, c_rows, batch, seq):
    t = q.shape[0]
    tq = ATTN_TILE
    nq = seq // tq
    seq_block = pl.BlockSpec((None, seq, BRANCH_W), lambda b, i: (b, 0, 0))
    return pl.pallas_call(
        _fox_kernel,
        grid=(batch, nq),
        in_specs=[pl.BlockSpec((tq, BRANCH_W), lambda b, i: (b * nq + i, 0)), seq_block, seq_block,
                  pl.BlockSpec((None, N_HEADS, seq), lambda b, i: (b, 0, 0))],
        out_specs=pl.BlockSpec((tq, BRANCH_W), lambda b, i: (b * nq + i, 0)),
        out_shape=jax.ShapeDtypeStruct((t, BRANCH_W), BF16),
        compiler_params=_params("parallel", "arbitrary"),
        name="fox_attention",
    )(q, k.reshape(batch, seq, BRANCH_W), v.reshape(batch, seq, BRANCH_W), c_rows)


def _dilated_kernel(q_ref, k_ref, v_ref, o_ref):
    tq = q_ref.shape[0]
    i = pl.program_id(1)
    row = lax.broadcasted_iota(jnp.int32, (tq, tq), 0)
    col = lax.broadcasted_iota(jnp.int32, (tq, tq), 1)
    diff = row - col
    far_bias = jnp.where(diff % DIL_FAR_STEP == 0, 0.0, NEG_INF)
    near_blocks = -(-DIL_MID_WINDOW // tq)

    def near_bias(delta):
        dist = diff + delta * tq
        ok = dist >= 0
        count = ((ok & (dist <= DIL_NEAR_WINDOW)).astype(F32)
                 + (ok & (dist <= DIL_MID_WINDOW) & (dist % DIL_MID_STEP == 0)).astype(F32)
                 + (ok & (dist % DIL_FAR_STEP == 0)).astype(F32))
        return jnp.log(count)

    near = [near_bias(delta) for delta in range(near_blocks + 1)]
    outs = []
    for h in range(N_HEADS):
        q = _head(q_ref, slice(None), h)

        def block(j, carry, bias):
            rows = pl.ds(pl.multiple_of(j * tq, tq), tq)
            return _flash_step(carry, q, _head(k_ref, rows, h), _head(v_ref, rows, h), bias)

        carry = lax.fori_loop(0, jnp.maximum(i - near_blocks, 0),
                              lambda j, c: block(j, c, far_bias), _flash_init(tq))
        for delta in range(near_blocks, 0, -1):
            carry = lax.cond(i >= delta, lambda c, d=delta: block(i - d, c, near[d]), lambda c: c, carry)
        _, l, acc = block(i, carry, near[0])
        outs.append(acc / l)
    o_ref[...] = jnp.concatenate(outs, axis=1).astype(BF16)


def _dilated(q, k, v, batch, seq):
    t = q.shape[0]
    tq = ATTN_TILE
    nq = seq // tq
    seq_block = pl.BlockSpec((None, seq, BRANCH_W), lambda b, i: (b, 0, 0))
    return pl.pallas_call(
        _dilated_kernel,
        grid=(batch, nq),
        in_specs=[pl.BlockSpec((tq, BRANCH_W), lambda b, i: (b * nq + i, 0)), seq_block, seq_block],
        out_specs=pl.BlockSpec((tq, BRANCH_W), lambda b, i: (b * nq + i, 0)),
        out_shape=jax.ShapeDtypeStruct((t, BRANCH_W), BF16),
        compiler_params=_params("parallel", "arbitrary"),
        name="dilated_attention",
    )(q, k.reshape(batch, seq, BRANCH_W), v.reshape(batch, seq, BRANCH_W))


def _swa_kernel(sink_ref, q_ref, kp_ref, kc_ref, vp_ref, vc_ref, o_ref):
    tq = q_ref.shape[0]
    i = pl.program_id(1)
    row = lax.broadcasted_iota(jnp.int32, (tq, 2 * tq), 0)
    col = lax.broadcasted_iota(jnp.int32, (tq, 2 * tq), 1)
    dist = row + tq - col
    valid = (dist >= 0) & (dist < SWA_WINDOW) & ((col >= tq) | (i > 0))
    bias = jnp.where(valid, 0.0, NEG_INF)
    group = N_HEADS // SWA_KV_HEADS
    outs = []
    for g in range(SWA_KV_HEADS):
        k = jnp.concatenate([_head(kp_ref, slice(None), g), _head(kc_ref, slice(None), g)], axis=0)
        v = jnp.concatenate([_head(vp_ref, slice(None), g), _head(vc_ref, slice(None), g)], axis=0)
        for hh in range(group):
            h = g * group + hh
            s = _mm_nt(_head(q_ref, slice(None), h), k) + bias
            sink = sink_ref[h]
            m = jnp.maximum(jnp.max(s, axis=-1, keepdims=True), sink)
            p = jnp.exp(s - m)
            l = jnp.sum(p, axis=-1, keepdims=True) + jnp.exp(sink - m)
            outs.append(_mm(p.astype(BF16), v) / l)
    o_ref[...] = jnp.concatenate(outs, axis=1).astype(BF16)


def _swa(q, k, v, sink, batch, seq):
    t = q.shape[0]
    tq = SWA_WINDOW
    nq = seq // tq
    cur = pl.BlockSpec((tq, SWA_KW), lambda b, i: (b * nq + i, 0))
    prev = pl.BlockSpec((tq, SWA_KW), lambda b, i: (b * nq + jnp.maximum(i - 1, 0), 0))
    return pl.pallas_call(
        _swa_kernel,
        grid=(batch, nq),
        in_specs=[pl.BlockSpec(memory_space=pltpu.SMEM),
                  pl.BlockSpec((tq, BRANCH_W), lambda b, i: (b * nq + i, 0)), prev, cur, prev, cur],
        out_specs=pl.BlockSpec((tq, BRANCH_W), lambda b, i: (b * nq + i, 0)),
        out_shape=jax.ShapeDtypeStruct((t, BRANCH_W), BF16),
        compiler_params=_params("parallel", "arbitrary"),
        name="swa_attention",
    )(sink, q, k, k, v, v)


def _outproj_kernel(h_ref, ya_ref, yb_ref, wa_ref, wb_ref, o_ref):
    o_ref[...] = h_ref[...] + _mm(ya_ref[...], wa_ref[...]) + _mm(yb_ref[...], wb_ref[...])


def _outproj(h, ya, yb, wa, wb):
    t = h.shape[0]
    return pl.pallas_call(
        _outproj_kernel,
        grid=(t // ROW_TILE,),
        in_specs=[_rows(ROW_TILE, D_MODEL), _rows(ROW_TILE, BRANCH_W), _rows(ROW_TILE, BRANCH_W),
                  _resident(wa.shape), _resident(wb.shape)],
        out_specs=_rows(ROW_TILE, D_MODEL),
        out_shape=jax.ShapeDtypeStruct((t, D_MODEL), F32),
        compiler_params=_params("parallel"),
        name="mixer_outproj",
    )(h, ya, yb, wa, wb)


def _memkv_kernel(x_ref, g_ref, w_ref, o_ref):
    xn = _rmsnorm(x_ref[...], g_ref[...]).astype(BF16)
    o_ref[...] = _mm(xn, w_ref[...]).astype(BF16)


def _memkv(mem, g, w):
    t = mem.shape[0]
    return pl.pallas_call(
        _memkv_kernel,
        grid=(t // ROW_TILE,),
        in_specs=[_rows(ROW_TILE, D_MODEL), _resident(g.shape), _resident(w.shape)],
        out_specs=_rows(ROW_TILE, 2 * D_MODEL),
        out_shape=jax.ShapeDtypeStruct((t, 2 * D_MODEL), BF16),
        compiler_params=_params("parallel"),
        name="memory_kv",
    )(mem, g, w)


def _xattn_kernel(h_ref, g_ref, wq_ref, kv_ref, wo_ref, o_ref):
    h = h_ref[...]
    xn = _rmsnorm(h, g_ref[...]).astype(BF16)
    q = (_mm(xn, wq_ref[...]) * XA_SCALE).astype(BF16)
    outs = []
    for hd in range(XA_HEADS):
        lo = hd * XA_HEAD_DIM
        s = _mm_nt(q[:, lo:lo + XA_HEAD_DIM], kv_ref[:, lo:lo + XA_HEAD_DIM])
        p = jnp.exp(s - jnp.max(s, axis=-1, keepdims=True))
        l = jnp.sum(p, axis=-1, keepdims=True)
        v = kv_ref[:, D_MODEL + lo:D_MODEL + lo + XA_HEAD_DIM]
        outs.append((_mm(p.astype(BF16), v) / l).astype(BF16))
    o_ref[...] = h + _mm(jnp.concatenate(outs, axis=1), wo_ref[...])


def _xattn(h, g, wq, kv, wo, seq):
    t = h.shape[0]
    tiles_per_seq = seq // ROW_TILE
    return pl.pallas_call(
        _xattn_kernel,
        grid=(t // ROW_TILE,),
        in_specs=[_rows(ROW_TILE, D_MODEL), _resident(g.shape), _resident(wq.shape),
                  pl.BlockSpec((None, MEM_LEN, 2 * D_MODEL), lambda i: (i // tiles_per_seq, 0, 0)),
                  _resident(wo.shape)],
        out_specs=_rows(ROW_TILE, D_MODEL),
        out_shape=jax.ShapeDtypeStruct((t, D_MODEL), F32),
        compiler_params=_params("parallel"),
        name="memory_xattn",
    )(h, g, wq, kv, wo)


FF_CHUNK = 512


def _mlp_kernel(h_ref, g_ref, wup_ref, wdown_ref, gout_ref, o_ref, *, final_norm):
    h = h_ref[...]
    xn = _rmsnorm(h, g_ref[...]).astype(BF16)
    acc = h
    for c in range(D_FF // FF_CHUNK):
        cols = slice(c * FF_CHUNK, (c + 1) * FF_CHUNK)
        up = jnp.maximum(_mm(xn, wup_ref[:, cols]), 0.0)
        acc = acc + _mm((up * up).astype(BF16), wdown_ref[cols, :])
    if final_norm:
        acc = _rmsnorm(acc, gout_ref[...])
    o_ref[...] = acc


def _mlp(h, g, wup, wdown, gout, final_norm):
    t = h.shape[0]
    return pl.pallas_call(
        functools.partial(_mlp_kernel, final_norm=final_norm),
        grid=(t // ROW_TILE,),
        in_specs=[_rows(ROW_TILE, D_MODEL), _resident(g.shape), _resident(wup.shape), _resident(wdown.shape),
                  _resident(gout.shape)],
        out_specs=_rows(ROW_TILE, D_MODEL),
        out_shape=jax.ShapeDtypeStruct((t, D_MODEL), F32),
        compiler_params=_params("parallel"),
        name="relu2_mlp",
    )(h, g, wup, wdown, gout)


def _block_diag(w):
    g, c, d = w.shape
    eye = jnp.eye(g, dtype=w.dtype)
    return (w[:, :, None, :] * eye[:, None, :, None]).reshape(g * c, g * d)


def _rope_tables(seq):
    half = HEAD_DIM // 2
    inv = ROPE_THETA ** (-jnp.arange(half, dtype=F32) / half)
    ang = jnp.arange(seq, dtype=F32)[:, None] * inv[None, :]
    reps = LANES // half
    cos = jnp.tile(jnp.cos(ang), (1, reps))
    sign = jnp.tile(jnp.concatenate([-jnp.ones((half,), F32), jnp.ones((half,), F32)]), LANES // HEAD_DIM)
    sin = jnp.tile(jnp.sin(ang), (1, reps)) * sign[None, :]
    return cos, sin


def _row(v, width=None):
    v = v.astype(F32).reshape(1, -1)
    if width is not None and v.shape[1] < width:
        v = jnp.pad(v, ((0, 0), (0, width - v.shape[1])))
    return v


def kernel(x, mem, ab_norm, ab_w_in, ab_conv_w, ab_conv_b, lru_w_a, lru_b_a, lru_w_i, lru_b_i, lru_lambda, fox_b_f, ab_w_out, cd_norm, cd_w_in, cd_sink, cd_w_out, xa_norm, xa_mem_norm, xa_w_q, xa_w_kv, xa_w_o, mlp_norm, mlp_w_up, mlp_w_down, final_norm):
    batch, seq, d = x.shape
    assert d == D_MODEL and seq % ROW_TILE == 0 and seq % ATTN_TILE == 0 and mem.shape[1] == MEM_LEN
    depth = xa_norm.shape[0]
    h = x.reshape(batch * seq, d)
    mem_rows = mem.reshape(batch * MEM_LEN, d)
    cos, sin = _rope_tables(seq)
    main_w = 5 * BRANCH_W

    for layer in range(depth):
        j = layer // 2
        if layer % 2 == 0:
            w_in = ab_w_in[j]
            w_f = jnp.pad(w_in[:, main_w:], ((0, 0), (0, LANES - N_HEADS))).astype(BF16)
            u, gate, q, k, v, f = _inproj_ab(h, _row(ab_norm[j]), w_in[:, :main_w].astype(BF16), w_f)
            ya, c = _lru(u, gate, f, ab_conv_w[j].astype(F32), _row(ab_conv_b[j]),
                         _block_diag(lru_w_a[j]).astype(BF16), _row(lru_b_a[j]),
                         _block_diag(lru_w_i[j]).astype(BF16), _row(lru_b_i[j]),
                         _row(lru_lambda[j]), _row(fox_b_f[j], LANES), batch, seq)
            c_rows = c.reshape(batch, seq, LANES)[:, :, :N_HEADS].transpose(0, 2, 1)
            yb = _fox(q, k, v, c_rows, batch, seq)
            w_out = ab_w_out[j].astype(BF16)
        else:
            qc, kc, vc, qd, kd, vd = _inproj_cd(h, _row(cd_norm[j]), cd_w_in[j].astype(BF16), cos, sin, seq)
            ya = _dilated(qc, kc, vc, batch, seq)
            yb = _swa(qd, kd, vd, cd_sink[j].astype(F32), batch, seq)
            w_out = cd_w_out[j].astype(BF16)
        h = _outproj(h, ya, yb, w_out[:BRANCH_W], w_out[BRANCH_W:])
        kv = _memkv(mem_rows, _row(xa_mem_norm[layer]), xa_w_kv[layer].astype(BF16))
        h = _xattn(h, _row(xa_norm[layer]), xa_w_q[layer].astype(BF16),
                   kv.reshape(batch, MEM_LEN, 2 * D_MODEL), xa_w_o[layer].astype(BF16), seq)
        h = _mlp(h, _row(mlp_norm[layer]), mlp_w_up[layer].astype(BF16), mlp_w_down[layer].astype(BF16),
                 _row(final_norm), final_norm=(layer == depth - 1))
    return h.reshape(batch, seq, d)
```

```python
import functools
import math

import jax
import jax.numpy as jnp
from jax import lax
from jax.experimental import pallas as pl
from jax.experimental.pallas import tpu as pltpu

F32 = jnp.float32
BF16 = jnp.bfloat16

D_MODEL = 1024
HEAD_DIM = 64
N_HEADS = 8
BRANCH_W = N_HEADS * HEAD_DIM
LRU_WIDTH = 512
LRU_BLOCKS = 8
CONV_WIDTH = 4
LRU_C = 8.0
SWA_KV_HEADS = 2
SWA_KW = SWA_KV_HEADS * HEAD_DIM
SWA_WINDOW = 128
DIL_NEAR_WINDOW = 128
DIL_MID_WINDOW = 512
DIL_MID_STEP = 4
DIL_FAR_STEP = 16
MEM_LEN = 256
XA_HEADS = 4
XA_HEAD_DIM = D_MODEL // XA_HEADS
D_FF = 4 * D_MODEL
ROPE_THETA = 10000.0
EPS = 1e-6
LOG2E = math.log2(math.e)
ATTN_SCALE = HEAD_DIM ** -0.5
Q_SCALE_LOG2 = ATTN_SCALE * LOG2E
XA_SCALE = XA_HEAD_DIM ** -0.5
LANES = 128
NEG_INF = float("-inf")

ROW_TILE = 512
SCAN_TILE = 256
ATTN_TILE = 256
VMEM_LIMIT = 48 * 1024 * 1024


def _params(*sem):
    return pltpu.CompilerParams(dimension_semantics=sem, vmem_limit_bytes=VMEM_LIMIT)


def _resident(shape):
    zeros = (0,) * len(shape)
    return pl.BlockSpec(shape, lambda *_: zeros, pipeline_mode=pl.Buffered(1))


def _rows(tile, width):
    return pl.BlockSpec((tile, width), lambda i: (i, 0))


def _rmsnorm(x, g):
    return x * lax.rsqrt(jnp.mean(x * x, axis=-1, keepdims=True) + EPS) * g


def _mm(a, b):
    return jnp.dot(a, b, preferred_element_type=F32)


def _mm_nt(a, b):
    return lax.dot_general(a, b, (((1,), (1,)), ((), ())), preferred_element_type=F32)


def _inproj_ab_kernel(x_ref, g_ref, w_ref, wf_ref, u_ref, gate_ref, q_ref, k_ref, v_ref, f_ref):
    xn = _rmsnorm(x_ref[...], g_ref[...]).astype(BF16)
    w = BRANCH_W
    u_ref[...] = _mm(xn, w_ref[:, 0:w])
    gate_ref[...] = _mm(xn, w_ref[:, w:2 * w])
    q_ref[...] = (_mm(xn, w_ref[:, 2 * w:3 * w]) * Q_SCALE_LOG2).astype(BF16)
    k_ref[...] = _mm(xn, w_ref[:, 3 * w:4 * w]).astype(BF16)
    v_ref[...] = _mm(xn, w_ref[:, 4 * w:5 * w]).astype(BF16)
    f_ref[...] = _mm(xn, wf_ref[...])


def _inproj_ab(h, g, w_main, w_f):
    t = h.shape[0]
    sds = jax.ShapeDtypeStruct
    return pl.pallas_call(
        _inproj_ab_kernel,
        grid=(t // ROW_TILE,),
        in_specs=[_rows(ROW_TILE, D_MODEL), _resident(g.shape), _resident(w_main.shape), _resident(w_f.shape)],
        out_specs=[_rows(ROW_TILE, BRANCH_W)] * 5 + [_rows(ROW_TILE, LANES)],
        out_shape=[sds((t, BRANCH_W), F32), sds((t, BRANCH_W), F32), sds((t, BRANCH_W), BF16),
                   sds((t, BRANCH_W), BF16), sds((t, BRANCH_W), BF16), sds((t, LANES), F32)],
        compiler_params=_params("parallel"),
        name="inproj_ab",
    )(h, g, w_main, w_f)


def _inproj_cd_kernel(x_ref, g_ref, w_ref, cos_ref, sin_ref, qc_ref, kc_ref, vc_ref, qd_ref, kd_ref, vd_ref):
    xn = _rmsnorm(x_ref[...], g_ref[...]).astype(BF16)
    cos = cos_ref[...]
    sin = sin_ref[...]
    lane = lax.broadcasted_iota(jnp.int32, cos.shape, 1)
    first_half = (lane % HEAD_DIM) < (HEAD_DIM // 2)

    def rope(z, scale):
        outs = []
        for c in range(z.shape[1] // LANES):
            zc = z[:, c * LANES:(c + 1) * LANES]
            partner = jnp.where(first_half, pltpu.roll(zc, LANES - HEAD_DIM // 2, 1), pltpu.roll(zc, HEAD_DIM // 2, 1))
            outs.append(((zc * cos + partner * sin) * scale).astype(BF16))
        return outs[0] if len(outs) == 1 else jnp.concatenate(outs, axis=1)

    w = BRANCH_W
    qc_ref[...] = rope(_mm(xn, w_ref[:, 0:w]), Q_SCALE_LOG2)
    kc_ref[...] = rope(_mm(xn, w_ref[:, w:2 * w]), 1.0)
    vc_ref[...] = _mm(xn, w_ref[:, 2 * w:3 * w]).astype(BF16)
    qd_ref[...] = rope(_mm(xn, w_ref[:, 3 * w:4 * w]), Q_SCALE_LOG2)
    kd_ref[...] = rope(_mm(xn, w_ref[:, 4 * w:4 * w + SWA_KW]), 1.0)
    vd_ref[...] = _mm(xn, w_ref[:, 4 * w + SWA_KW:4 * w + 2 * SWA_KW]).astype(BF16)


def _inproj_cd(h, g, w, cos, sin, seq):
    t = h.shape[0]
    sds = jax.ShapeDtypeStruct
    tiles_per_seq = seq // ROW_TILE
    table = pl.BlockSpec((ROW_TILE, LANES), lambda i: (i % tiles_per_seq, 0))
    return pl.pallas_call(
        _inproj_cd_kernel,
        grid=(t // ROW_TILE,),
        in_specs=[_rows(ROW_TILE, D_MODEL), _resident(g.shape), _resident(w.shape), table, table],
        out_specs=[_rows(ROW_TILE, BRANCH_W)] * 4 + [_rows(ROW_TILE, SWA_KW)] * 2,
        out_shape=[sds((t, BRANCH_W), BF16)] * 4 + [sds((t, SWA_KW), BF16)] * 2,
        compiler_params=_params("parallel"),
        name="inproj_cd",
    )(h, g, w, cos, sin)


def _scan_linear(a, x):
    n = a.shape[0]
    row = lax.broadcasted_iota(jnp.int32, a.shape, 0)
    d = 1
    while d < n:
        keep = row >= d
        x = jnp.where(keep, x + a * pltpu.roll(x, d, 0), x)
        a = jnp.where(keep, a * pltpu.roll(a, d, 0), a)
        d *= 2
    return a, x


def _scan_sum(x):
    n = x.shape[0]
    row = lax.broadcasted_iota(jnp.int32, x.shape, 0)
    d = 1
    while d < n:
        x = jnp.where(row >= d, x + pltpu.roll(x, d, 0), x)
        d *= 2
    return x


def _softplus(x):
    return jnp.maximum(x, 0.0) + jnp.log1p(jnp.exp(-jnp.abs(x)))


def _sigmoid(x):
    return 1.0 / (1.0 + jnp.exp(-x))


def _gelu_tanh(x):
    return 0.5 * x * (1.0 + jnp.tanh(math.sqrt(2.0 / math.pi) * (x + 0.044715 * (x * x * x))))


def _lru_kernel(u_ref, gate_ref, f_ref, cw_ref, cb_ref, wa_ref, ba_ref, wi_ref, bi_ref, lam_ref, bf_ref,
                ya_ref, c_ref, ubuf, hprev, cprev):
    tc = u_ref.shape[0]
    tail = CONV_WIDTH * 2

    @pl.when(pl.program_id(1) == 0)
    def _():
        ubuf[0:tail, :] = jnp.zeros((tail, LRU_WIDTH), F32)
        hprev[...] = jnp.zeros_like(hprev)
        cprev[...] = jnp.zeros_like(cprev)

    u = u_ref[...]
    ubuf[tail:tail + tc, :] = u
    cw = cw_ref[...]
    conv = cb_ref[...] + cw[CONV_WIDTH - 1:CONV_WIDTH, :] * u
    for back in range(1, CONV_WIDTH):
        k = CONV_WIDTH - 1 - back
        conv = conv + cw[k:k + 1, :] * ubuf[tail - back:tail - back + tc, :]
    ubuf[0:tail, :] = u[tc - tail:tc, :]

    conv_b = conv.astype(BF16)
    r = _sigmoid(_mm(conv_b, wa_ref[...]) + ba_ref[...])
    gate_i = _sigmoid(_mm(conv_b, wi_ref[...]) + bi_ref[...])
    log_a = (-LRU_C) * r * _softplus(-lam_ref[...])
    a = jnp.exp(log_a)
    x_in = jnp.sqrt(-jnp.tanh(log_a) * (a * a + 1.0)) * (gate_i * conv)
    a_cum, h_loc = _scan_linear(a, x_in)
    h = h_loc + a_cum * hprev[7:8, :]
    hprev[...] = h[tc - 8:tc, :]
    ya_ref[...] = (h * _gelu_tanh(gate_ref[...])).astype(BF16)

    z = f_ref[...] + bf_ref[...]
    log_f = jnp.minimum(z, 0.0) - jnp.log1p(jnp.exp(-jnp.abs(z)))
    c = _scan_sum(log_f) + cprev[7:8, :]
    cprev[...] = c[tc - 8:tc, :]
    c_ref[...] = c


def _lru(u, gate, f, conv_w, conv_b, wa, ba, wi, bi, lam, bf, batch, seq):
    t = u.shape[0]
    tc = SCAN_TILE
    nt = seq // tc
    sds = jax.ShapeDtypeStruct

    def rows(width):
        return pl.BlockSpec((tc, width), lambda b, j: (b * nt + j, 0))

    small = [conv_w, conv_b, wa, ba, wi, bi, lam, bf]
    return pl.pallas_call(
        _lru_kernel,
        grid=(batch, nt),
        in_specs=[rows(LRU_WIDTH), rows(LRU_WIDTH), rows(LANES)] + [_resident(a.shape) for a in small],
        out_specs=[rows(LRU_WIDTH), rows(LANES)],
        out_shape=[sds((t, LRU_WIDTH), BF16), sds((t, LANES), F32)],
        scratch_shapes=[pltpu.VMEM((tc + 2 * CONV_WIDTH, LRU_WIDTH), F32),
                        pltpu.VMEM((8, LRU_WIDTH), F32), pltpu.VMEM((8, LANES), F32)],
        compiler_params=_params("parallel", "arbitrary"),
        name="rglru",
    )(u, gate, f, *small)


HEAD_PAIRS = N_HEADS // 2


def _pair(ref, rows, p):
    return ref[rows, p * LANES:(p + 1) * LANES]


def _low_head(shape):
    return lax.broadcasted_iota(jnp.int32, shape, 1) < HEAD_DIM


def _split_pair(x):
    low = _low_head(x.shape)
    zero = jnp.zeros_like(x)
    return jnp.where(low, x, zero), jnp.where(low, zero, x)


def _pair_values(v, ones_ref):
    v_lo, v_hi = _split_pair(v)
    return jnp.concatenate([jnp.concatenate([v_lo, ones_ref[0]], axis=1),
                            jnp.concatenate([v_hi, ones_ref[1]], axis=1)], axis=0)


def _head_ones(keys):
    low = (jnp.arange(LANES) < HEAD_DIM).astype(BF16)
    return jnp.broadcast_to(jnp.stack([low, 1 - low])[:, None, :], (2, keys, LANES))


def _attn_begin(q_ref, qm_s, m_s, acc_s):
    for p in range(HEAD_PAIRS):
        qm_s[2 * p], qm_s[2 * p + 1] = _split_pair(_pair(q_ref, slice(None), p))
    m_s[...] = jnp.full(m_s.shape, NEG_INF, F32)
    acc_s[...] = jnp.zeros_like(acc_s)


def _attn_block(qm_s, k_ref, v_ref, ones_ref, m_s, acc_s, rows, width, bias_fn):
    tq = qm_s.shape[1]
    low_q = _low_head((tq, LANES))
    for p in range(HEAD_PAIRS):
        kp = _pair(k_ref, rows, p)
        rhs = _pair_values(_pair(v_ref, rows, p), ones_ref)
        probs, drops = [], []
        for e in range(2):
            h = 2 * p + e
            s = _mm_nt(qm_s[h], kp) + bias_fn(h)
            m_prev = m_s[h]
            m_new = jnp.maximum(m_prev, jnp.max(s, axis=-1, keepdims=True))
            m_s[h] = m_new
            probs.append(jnp.exp2(s - jnp.concatenate([m_new] * (width // LANES), axis=1)).astype(BF16))
            drops.append(m_prev - m_new)
        alpha = jnp.exp2(jnp.where(low_q, drops[0], drops[1]))
        acc_s[p] = acc_s[p] * jnp.concatenate([alpha, alpha], axis=1) + _mm(jnp.concatenate(probs, axis=1), rhs)


def _attn_finish(o_ref, acc_s):
    for p in range(HEAD_PAIRS):
        acc = acc_s[p]
        o_ref[:, p * LANES:(p + 1) * LANES] = (acc[:, :LANES] / acc[:, LANES:]).astype(BF16)


def _attn_scratch(tq):
    return [pltpu.VMEM((N_HEADS, tq, LANES), BF16), pltpu.VMEM((N_HEADS, tq, LANES), F32),
            pltpu.VMEM((HEAD_PAIRS, tq, 2 * LANES), F32)]


def _kv_rows(j, tq):
    return pl.ds(pl.multiple_of(j * tq, tq), tq)


def _fox_kernel(q_ref, k_ref, v_ref, ones_ref, c_ref, o_ref, qm_s, m_s, acc_s):
    tq = q_ref.shape[0]
    i = pl.program_id(1)
    state = (qm_s, k_ref, v_ref, ones_ref, m_s, acc_s)
    _attn_begin(q_ref, qm_s, m_s, acc_s)

    def full_block(j, _):
        rows = _kv_rows(j, tq)
        _attn_block(*state, rows, tq, lambda h: c_ref[h:h + 1, rows] * (-LOG2E))
        return 0

    lax.fori_loop(0, i, full_block, 0)
    row = lax.broadcasted_iota(jnp.int32, (tq, tq), 0)
    col = lax.broadcasted_iota(jnp.int32, (tq, tq), 1)
    causal = jnp.where(row >= col, 0.0, NEG_INF)
    rows = _kv_rows(i, tq)
    _attn_block(*state, rows, tq, lambda h: causal + c_ref[h:h + 1, rows] * (-LOG2E))
    _attn_finish(o_ref, acc_s)


def _fox(q, k, v, c_rows, batch, seq):
    t = q.shape[0]
    tq = ATTN_TILE
    nq = seq // tq
    seq_block = pl.BlockSpec((None, seq, BRANCH_W), lambda b, i: (b, 0, 0))
    return pl.pallas_call(
        _fox_kernel,
        grid=(batch, nq),
        in_specs=[pl.BlockSpec((tq, BRANCH_W), lambda b, i: (b * nq + i, 0)), seq_block, seq_block,
                  _resident((2, tq, LANES)), pl.BlockSpec((None, N_HEADS, seq), lambda b, i: (b, 0, 0))],
        out_specs=pl.BlockSpec((tq, BRANCH_W), lambda b, i: (b * nq + i, 0)),
        out_shape=jax.ShapeDtypeStruct((t, BRANCH_W), BF16),
        scratch_shapes=_attn_scratch(tq),
        compiler_params=_params("arbitrary", "arbitrary"),
        name="fox_attention",
    )(q, k.reshape(batch, seq, BRANCH_W), v.reshape(batch, seq, BRANCH_W), _head_ones(tq), c_rows)


DIL_NEAR_BLOCKS = -(-DIL_MID_WINDOW // ATTN_TILE)


def _dilated_kernel(q_ref, k_ref, v_ref, ones_ref, o_ref, qm_s, m_s, acc_s, bias_s):
    tq = q_ref.shape[0]
    i = pl.program_id(1)
    far = DIL_NEAR_BLOCKS + 1

    @pl.when((pl.program_id(0) == 0) & (i == 0))
    def _():
        row = lax.broadcasted_iota(jnp.int32, (tq, tq), 0)
        col = lax.broadcasted_iota(jnp.int32, (tq, tq), 1)
        diff = row - col
        for delta in range(far):
            dist = diff + delta * tq
            ok = dist >= 0
            count = ((ok & (dist <= DIL_NEAR_WINDOW)).astype(F32)
                     + (ok & (dist <= DIL_MID_WINDOW) & (dist % DIL_MID_STEP == 0)).astype(F32)
                     + (ok & (dist % DIL_FAR_STEP == 0)).astype(F32))
            bias_s[delta] = jnp.log2(count)
        bias_s[far] = jnp.where(diff % DIL_FAR_STEP == 0, 0.0, NEG_INF)

    state = (qm_s, k_ref, v_ref, ones_ref, m_s, acc_s)
    _attn_begin(q_ref, qm_s, m_s, acc_s)
    _attn_block(*state, _kv_rows(i, tq), tq, lambda h: bias_s[0])
    for delta in range(1, far):
        @pl.when(i >= delta)
        def _(delta=delta):
            _attn_block(*state, _kv_rows(i - delta, tq), tq, lambda h: bias_s[delta])

    def far_block(j, _):
        _attn_block(*state, _kv_rows(j, tq), tq, lambda h: bias_s[far])
        return 0

    lax.fori_loop(0, jnp.maximum(i - DIL_NEAR_BLOCKS, 0), far_block, 0)
    _attn_finish(o_ref, acc_s)


def _dilated(q, k, v, batch, seq):
    t = q.shape[0]
    tq = ATTN_TILE
    nq = seq // tq
    seq_block = pl.BlockSpec((None, seq, BRANCH_W), lambda b, i: (b, 0, 0))
    return pl.pallas_call(
        _dilated_kernel,
        grid=(batch, nq),
        in_specs=[pl.BlockSpec((tq, BRANCH_W), lambda b, i: (b * nq + i, 0)), seq_block, seq_block,
                  _resident((2, tq, LANES))],
        out_specs=pl.BlockSpec((tq, BRANCH_W), lambda b, i: (b * nq + i, 0)),
        out_shape=jax.ShapeDtypeStruct((t, BRANCH_W), BF16),
        scratch_shapes=_attn_scratch(tq) + [pltpu.VMEM((DIL_NEAR_BLOCKS + 2, tq, tq), F32)],
        compiler_params=_params("arbitrary", "arbitrary"),
        name="dilated_attention",
    )(q, k.reshape(batch, seq, BRANCH_W), v.reshape(batch, seq, BRANCH_W), _head_ones(tq))


SWA_GROUP = N_HEADS // SWA_KV_HEADS
SWA_HEAD_ORDER = tuple(g * SWA_GROUP + p for p in range(SWA_GROUP) for g in range(SWA_KV_HEADS))


def _swa_kernel(sink_ref, q_ref, kp_ref, kc_ref, vp_ref, vc_ref, ones_ref, o_ref):
    tq = q_ref.shape[0]
    i = pl.program_id(1)
    row = lax.broadcasted_iota(jnp.int32, (tq, 2 * tq), 0)
    col = lax.broadcasted_iota(jnp.int32, (tq, 2 * tq), 1)
    dist = row + tq - col
    valid = (dist >= 0) & (dist < SWA_WINDOW) & ((col >= tq) | (i > 0))
    bias = jnp.where(valid, 0.0, NEG_INF)
    k = jnp.concatenate([kp_ref[...], kc_ref[...]], axis=0)
    rhs = _pair_values(jnp.concatenate([vp_ref[...], vc_ref[...]], axis=0), ones_ref)
    low_q = _low_head((tq, LANES))
    for p in range(SWA_GROUP):
        probs, sink_drop = [], []
        for g, qm in enumerate(_split_pair(_pair(q_ref, slice(None), p))):
            sink = sink_ref[g * SWA_GROUP + p] * LOG2E
            s = _mm_nt(qm, k) + bias
            m = jnp.maximum(jnp.max(s, axis=-1, keepdims=True), sink)
            probs.append(jnp.exp2(s - m).astype(BF16))
            sink_drop.append(sink - m)
        out = _mm(jnp.concatenate(probs, axis=1), rhs)
        den = out[:, LANES:] + jnp.exp2(jnp.where(low_q, sink_drop[0], sink_drop[1]))
        o_ref[:, p * LANES:(p + 1) * LANES] = (out[:, :LANES] / den).astype(BF16)


def _swa(q, k, v, sink, batch, seq):
    t = q.shape[0]
    tq = SWA_WINDOW
    nq = seq // tq
    cur = pl.BlockSpec((tq, SWA_KW), lambda b, i: (b * nq + i, 0))
    prev = pl.BlockSpec((tq, SWA_KW), lambda b, i: (b * nq + jnp.maximum(i - 1, 0), 0))
    return pl.pallas_call(
        _swa_kernel,
        grid=(batch, nq),
        in_specs=[pl.BlockSpec(memory_space=pltpu.SMEM),
                  pl.BlockSpec((tq, BRANCH_W), lambda b, i: (b * nq + i, 0)), prev, cur, prev, cur,
                  _resident((2, 2 * tq, LANES))],
        out_specs=pl.BlockSpec((tq, BRANCH_W), lambda b, i: (b * nq + i, 0)),
        out_shape=jax.ShapeDtypeStruct((t, BRANCH_W), BF16),
        compiler_params=_params("parallel", "arbitrary"),
        name="swa_attention",
    )(sink, q, k, k, v, v, _head_ones(2 * tq))


def _outproj_kernel(h_ref, ya_ref, yb_ref, wa_ref, wb_ref, o_ref):
    o_ref[...] = h_ref[...] + _mm(ya_ref[...], wa_ref[...]) + _mm(yb_ref[...], wb_ref[...])


def _outproj(h, ya, yb, wa, wb):
    t = h.shape[0]
    return pl.pallas_call(
        _outproj_kernel,
        grid=(t // ROW_TILE,),
        in_specs=[_rows(ROW_TILE, D_MODEL), _rows(ROW_TILE, BRANCH_W), _rows(ROW_TILE, BRANCH_W),
                  _resident(wa.shape), _resident(wb.shape)],
        out_specs=_rows(ROW_TILE, D_MODEL),
        out_shape=jax.ShapeDtypeStruct((t, D_MODEL), F32),
        compiler_params=_params("parallel"),
        name="mixer_outproj",
    )(h, ya, yb, wa, wb)


def _memkv_kernel(x_ref, g_ref, w_ref, o_ref):
    xn = _rmsnorm(x_ref[...], g_ref[...]).astype(BF16)
    o_ref[...] = _mm(xn, w_ref[...]).astype(BF16)


def _memkv(mem, g, w):
    t = mem.shape[0]
    return pl.pallas_call(
        _memkv_kernel,
        grid=(t // ROW_TILE,),
        in_specs=[_rows(ROW_TILE, D_MODEL), _resident(g.shape), _resident(w.shape)],
        out_specs=_rows(ROW_TILE, 2 * D_MODEL),
        out_shape=jax.ShapeDtypeStruct((t, 2 * D_MODEL), BF16),
        compiler_params=_params("parallel"),
        name="memory_kv",
    )(mem, g, w)


def _xattn_kernel(h_ref, g_ref, wq_ref, kv_ref, wo_ref, o_ref):
    h = h_ref[...]
    xn = _rmsnorm(h, g_ref[...]).astype(BF16)
    q = (_mm(xn, wq_ref[...]) * XA_SCALE).astype(BF16)
    outs = []
    for hd in range(XA_HEADS):
        lo = hd * XA_HEAD_DIM
        s = _mm_nt(q[:, lo:lo + XA_HEAD_DIM], kv_ref[:, lo:lo + XA_HEAD_DIM])
        p = jnp.exp(s - jnp.max(s, axis=-1, keepdims=True))
        l = jnp.sum(p, axis=-1, keepdims=True)
        v = kv_ref[:, D_MODEL + lo:D_MODEL + lo + XA_HEAD_DIM]
        outs.append((_mm(p.astype(BF16), v) / l).astype(BF16))
    o_ref[...] = h + _mm(jnp.concatenate(outs, axis=1), wo_ref[...])


def _xattn(h, g, wq, kv, wo, seq):
    t = h.shape[0]
    tiles_per_seq = seq // ROW_TILE
    return pl.pallas_call(
        _xattn_kernel,
        grid=(t // ROW_TILE,),
        in_specs=[_rows(ROW_TILE, D_MODEL), _resident(g.shape), _resident(wq.shape),
                  pl.BlockSpec((None, MEM_LEN, 2 * D_MODEL), lambda i: (i // tiles_per_seq, 0, 0)),
                  _resident(wo.shape)],
        out_specs=_rows(ROW_TILE, D_MODEL),
        out_shape=jax.ShapeDtypeStruct((t, D_MODEL), F32),
        compiler_params=_params("parallel"),
        name="memory_xattn",
    )(h, g, wq, kv, wo)


FF_CHUNK = 512


def _mlp_kernel(h_ref, g_ref, wup_ref, wdown_ref, gout_ref, o_ref, *, final_norm):
    h = h_ref[...]
    xn = _rmsnorm(h, g_ref[...]).astype(BF16)
    acc = h
    for c in range(D_FF // FF_CHUNK):
        cols = slice(c * FF_CHUNK, (c + 1) * FF_CHUNK)
        up = jnp.maximum(_mm(xn, wup_ref[:, cols]), 0.0)
        acc = acc + _mm((up * up).astype(BF16), wdown_ref[cols, :])
    if final_norm:
        acc = _rmsnorm(acc, gout_ref[...])
    o_ref[...] = acc


def _mlp(h, g, wup, wdown, gout, final_norm):
    t = h.shape[0]
    return pl.pallas_call(
        functools.partial(_mlp_kernel, final_norm=final_norm),
        grid=(t // ROW_TILE,),
        in_specs=[_rows(ROW_TILE, D_MODEL), _resident(g.shape), _resident(wup.shape), _resident(wdown.shape),
                  _resident(gout.shape)],
        out_specs=_rows(ROW_TILE, D_MODEL),
        out_shape=jax.ShapeDtypeStruct((t, D_MODEL), F32),
        compiler_params=_params("parallel"),
        name="relu2_mlp",
    )(h, g, wup, wdown, gout)


def _block_diag(w):
    g, c, d = w.shape
    eye = jnp.eye(g, dtype=w.dtype)
    return (w[:, :, None, :] * eye[:, None, :, None]).reshape(g * c, g * d)


def _rope_tables(seq):
    half = HEAD_DIM // 2
    inv = ROPE_THETA ** (-jnp.arange(half, dtype=F32) / half)
    ang = jnp.arange(seq, dtype=F32)[:, None] * inv[None, :]
    reps = LANES // half
    cos = jnp.tile(jnp.cos(ang), (1, reps))
    sign = jnp.tile(jnp.concatenate([-jnp.ones((half,), F32), jnp.ones((half,), F32)]), LANES // HEAD_DIM)
    sin = jnp.tile(jnp.sin(ang), (1, reps)) * sign[None, :]
    return cos, sin


def _permute_heads(w, axis):
    shape = w.shape
    split = shape[:axis] + (N_HEADS, HEAD_DIM) + shape[axis + 1:]
    return jnp.take(w.reshape(split), jnp.array(SWA_HEAD_ORDER), axis=axis).reshape(shape)


def _row(v, width=None):
    v = v.astype(F32).reshape(1, -1)
    if width is not None and v.shape[1] < width:
        v = jnp.pad(v, ((0, 0), (0, width - v.shape[1])))
    return v


def kernel(x, mem, ab_norm, ab_w_in, ab_conv_w, ab_conv_b, lru_w_a, lru_b_a, lru_w_i, lru_b_i, lru_lambda, fox_b_f, ab_w_out, cd_norm, cd_w_in, cd_sink, cd_w_out, xa_norm, xa_mem_norm, xa_w_q, xa_w_kv, xa_w_o, mlp_norm, mlp_w_up, mlp_w_down, final_norm):
    batch, seq, d = x.shape
    assert d == D_MODEL and seq % ROW_TILE == 0 and seq % ATTN_TILE == 0 and mem.shape[1] == MEM_LEN
    depth = xa_norm.shape[0]
    h = x.reshape(batch * seq, d)
    mem_rows = mem.reshape(batch * MEM_LEN, d)
    cos, sin = _rope_tables(seq)
    main_w = 5 * BRANCH_W

    for layer in range(depth):
        j = layer // 2
        if layer % 2 == 0:
            w_in = ab_w_in[j]
            w_f = jnp.pad(w_in[:, main_w:], ((0, 0), (0, LANES - N_HEADS))).astype(BF16)
            u, gate, q, k, v, f = _inproj_ab(h, _row(ab_norm[j]), w_in[:, :main_w].astype(BF16), w_f)
            ya, c = _lru(u, gate, f, ab_conv_w[j].astype(F32), _row(ab_conv_b[j]),
                         _block_diag(lru_w_a[j]).astype(BF16), _row(lru_b_a[j]),
                         _block_diag(lru_w_i[j]).astype(BF16), _row(lru_b_i[j]),
                         _row(lru_lambda[j]), _row(fox_b_f[j], LANES), batch, seq)
            c_rows = c.reshape(batch, seq, LANES)[:, :, :N_HEADS].transpose(0, 2, 1)
            yb = _fox(q, k, v, c_rows, batch, seq)
            w_out = ab_w_out[j].astype(BF16)
        else:
            w_in = cd_w_in[j]
            qd_lo, qd_hi = 3 * BRANCH_W, 4 * BRANCH_W
            w_in = jnp.concatenate([w_in[:, :qd_lo], _permute_heads(w_in[:, qd_lo:qd_hi], 1), w_in[:, qd_hi:]], axis=1)
            qc, kc, vc, qd, kd, vd = _inproj_cd(h, _row(cd_norm[j]), w_in.astype(BF16), cos, sin, seq)
            ya = _dilated(qc, kc, vc, batch, seq)
            yb = _swa(qd, kd, vd, cd_sink[j].astype(F32), batch, seq)
            w_out = cd_w_out[j]
            w_out = jnp.concatenate([w_out[:BRANCH_W], _permute_heads(w_out[BRANCH_W:], 0)], axis=0).astype(BF16)
        h = _outproj(h, ya, yb, w_out[:BRANCH_W], w_out[BRANCH_W:])
        kv = _memkv(mem_rows, _row(xa_mem_norm[layer]), xa_w_kv[layer].astype(BF16))
        h = _xattn(h, _row(xa_norm[layer]), xa_w_q[layer].astype(BF16),
                   kv.reshape(batch, MEM_LEN, 2 * D_MODEL), xa_w_o[layer].astype(BF16), seq)
        h = _mlp(h, _row(mlp_norm[layer]), mlp_w_up[layer].astype(BF16), mlp_w_down[layer].astype(BF16),
                 _row(final_norm), final_norm=(layer == depth - 1))
    return h.reshape(batch, seq, d)
```

```python
import functools
import math

import jax
import jax.numpy as jnp
from jax import lax
from jax.experimental import pallas as pl
from jax.experimental.pallas import tpu as pltpu

F32 = jnp.float32
BF16 = jnp.bfloat16

D_MODEL = 1024
HEAD_DIM = 64
N_HEADS = 8
BRANCH_W = N_HEADS * HEAD_DIM
LRU_WIDTH = 512
LRU_BLOCKS = 8
CONV_WIDTH = 4
LRU_C = 8.0
SWA_KV_HEADS = 2
SWA_KW = SWA_KV_HEADS * HEAD_DIM
SWA_WINDOW = 128
DIL_NEAR_WINDOW = 128
DIL_MID_WINDOW = 512
DIL_MID_STEP = 4
DIL_FAR_STEP = 16
MEM_LEN = 256
XA_HEADS = 4
XA_HEAD_DIM = D_MODEL // XA_HEADS
D_FF = 4 * D_MODEL
ROPE_THETA = 10000.0
EPS = 1e-6
LOG2E = math.log2(math.e)
ATTN_SCALE = HEAD_DIM ** -0.5
Q_SCALE_LOG2 = ATTN_SCALE * LOG2E
XA_SCALE = XA_HEAD_DIM ** -0.5
LANES = 128
NEG_INF = float("-inf")

ROW_TILE = 512
SCAN_TILE = 256
ATTN_TILE = 512
VMEM_LIMIT = 48 * 1024 * 1024


def _params(*sem):
    return pltpu.CompilerParams(dimension_semantics=sem, vmem_limit_bytes=VMEM_LIMIT)


def _resident(shape):
    zeros = (0,) * len(shape)
    return pl.BlockSpec(shape, lambda *_: zeros, pipeline_mode=pl.Buffered(1))


def _rows(tile, width):
    return pl.BlockSpec((tile, width), lambda i: (i, 0))


def _rmsnorm(x, g):
    return x * lax.rsqrt(jnp.mean(x * x, axis=-1, keepdims=True) + EPS) * g


def _mm(a, b):
    return jnp.dot(a, b, preferred_element_type=F32)


def _mm_nt(a, b):
    return lax.dot_general(a, b, (((1,), (1,)), ((), ())), preferred_element_type=F32)


def _inproj_ab_kernel(x_ref, g_ref, w_ref, wf_ref, u_ref, gate_ref, q_ref, k_ref, v_ref, f_ref):
    xn = _rmsnorm(x_ref[...], g_ref[...]).astype(BF16)
    w = BRANCH_W
    u_ref[...] = _mm(xn, w_ref[:, 0:w])
    gate_ref[...] = _mm(xn, w_ref[:, w:2 * w])
    q_ref[...] = (_mm(xn, w_ref[:, 2 * w:3 * w]) * Q_SCALE_LOG2).astype(BF16)
    k_ref[...] = _mm(xn, w_ref[:, 3 * w:4 * w]).astype(BF16)
    v_ref[...] = _mm(xn, w_ref[:, 4 * w:5 * w]).astype(BF16)
    f_ref[...] = _mm(xn, wf_ref[...])


def _inproj_ab(h, g, w_main, w_f):
    t = h.shape[0]
    sds = jax.ShapeDtypeStruct
    return pl.pallas_call(
        _inproj_ab_kernel,
        grid=(t // ROW_TILE,),
        in_specs=[_rows(ROW_TILE, D_MODEL), _resident(g.shape), _resident(w_main.shape), _resident(w_f.shape)],
        out_specs=[_rows(ROW_TILE, BRANCH_W)] * 5 + [_rows(ROW_TILE, LANES)],
        out_shape=[sds((t, BRANCH_W), F32), sds((t, BRANCH_W), F32), sds((t, BRANCH_W), BF16),
                   sds((t, BRANCH_W), BF16), sds((t, BRANCH_W), BF16), sds((t, LANES), F32)],
        compiler_params=_params("parallel"),
        name="inproj_ab",
    )(h, g, w_main, w_f)


def _inproj_cd_kernel(x_ref, g_ref, w_ref, cos_ref, sin_ref, qc_ref, kc_ref, vc_ref, qd_ref, kd_ref, vd_ref):
    xn = _rmsnorm(x_ref[...], g_ref[...]).astype(BF16)
    cos = cos_ref[...]
    sin = sin_ref[...]
    lane = lax.broadcasted_iota(jnp.int32, cos.shape, 1)
    first_half = (lane % HEAD_DIM) < (HEAD_DIM // 2)

    def rope(z, scale):
        outs = []
        for c in range(z.shape[1] // LANES):
            zc = z[:, c * LANES:(c + 1) * LANES]
            partner = jnp.where(first_half, pltpu.roll(zc, LANES - HEAD_DIM // 2, 1), pltpu.roll(zc, HEAD_DIM // 2, 1))
            outs.append(((zc * cos + partner * sin) * scale).astype(BF16))
        return outs[0] if len(outs) == 1 else jnp.concatenate(outs, axis=1)

    w = BRANCH_W
    qc_ref[...] = rope(_mm(xn, w_ref[:, 0:w]), Q_SCALE_LOG2)
    kc_ref[...] = rope(_mm(xn, w_ref[:, w:2 * w]), 1.0)
    vc_ref[...] = _mm(xn, w_ref[:, 2 * w:3 * w]).astype(BF16)
    qd_ref[...] = rope(_mm(xn, w_ref[:, 3 * w:4 * w]), Q_SCALE_LOG2)
    kd_ref[...] = rope(_mm(xn, w_ref[:, 4 * w:4 * w + SWA_KW]), 1.0)
    vd_ref[...] = _mm(xn, w_ref[:, 4 * w + SWA_KW:4 * w + 2 * SWA_KW]).astype(BF16)


def _inproj_cd(h, g, w, cos, sin, seq):
    t = h.shape[0]
    sds = jax.ShapeDtypeStruct
    tiles_per_seq = seq // ROW_TILE
    table = pl.BlockSpec((ROW_TILE, LANES), lambda i: (i % tiles_per_seq, 0))
    return pl.pallas_call(
        _inproj_cd_kernel,
        grid=(t // ROW_TILE,),
        in_specs=[_rows(ROW_TILE, D_MODEL), _resident(g.shape), _resident(w.shape), table, table],
        out_specs=[_rows(ROW_TILE, BRANCH_W)] * 4 + [_rows(ROW_TILE, SWA_KW)] * 2,
        out_shape=[sds((t, BRANCH_W), BF16)] * 4 + [sds((t, SWA_KW), BF16)] * 2,
        compiler_params=_params("parallel"),
        name="inproj_cd",
    )(h, g, w, cos, sin)


SUBLANES = 8


def _group_rows(x):
    return x.reshape(x.shape[0] // SUBLANES, SUBLANES, x.shape[1])


def _scan_linear(a, x, h0):
    a3, x3 = _group_rows(a), _group_rows(x)
    row = lax.broadcasted_iota(jnp.int32, a3.shape, 1)
    d = 1
    while d < SUBLANES:
        keep = row >= d
        x3 = jnp.where(keep, x3 + a3 * pltpu.roll(x3, d, 1), x3)
        a3 = jnp.where(keep, a3 * pltpu.roll(a3, d, 1), a3)
        d *= 2
    out, carry = [], h0
    for g in range(a3.shape[0]):
        hg = a3[g] * carry + x3[g]
        out.append(hg)
        carry = hg[SUBLANES - 1:SUBLANES, :]
    return jnp.concatenate(out, axis=0)


def _scan_sum(x, c0):
    x3 = _group_rows(x)
    row = lax.broadcasted_iota(jnp.int32, x3.shape, 1)
    d = 1
    while d < SUBLANES:
        x3 = jnp.where(row >= d, x3 + pltpu.roll(x3, d, 1), x3)
        d *= 2
    out, carry = [], c0
    for g in range(x3.shape[0]):
        cg = x3[g] + carry
        out.append(cg)
        carry = cg[SUBLANES - 1:SUBLANES, :]
    return jnp.concatenate(out, axis=0)


def _softplus(x):
    return jnp.maximum(x, 0.0) + jnp.log1p(jnp.exp(-jnp.abs(x)))


def _sigmoid(x):
    return 1.0 / (1.0 + jnp.exp(-x))


def _gelu_tanh(x):
    return 0.5 * x * (1.0 + jnp.tanh(math.sqrt(2.0 / math.pi) * (x + 0.044715 * (x * x * x))))


def _lru_kernel(u_ref, gate_ref, f_ref, cw_ref, cb_ref, wa_ref, ba_ref, wi_ref, bi_ref, lam_ref, bf_ref,
                ya_ref, c_ref, ubuf, hprev, cprev):
    tc = u_ref.shape[0]
    tail = CONV_WIDTH * 2

    @pl.when(pl.program_id(1) == 0)
    def _():
        ubuf[0:tail, :] = jnp.zeros((tail, LRU_WIDTH), F32)
        hprev[...] = jnp.zeros_like(hprev)
        cprev[...] = jnp.zeros_like(cprev)

    u = u_ref[...]
    ubuf[tail:tail + tc, :] = u
    cw = cw_ref[...]
    conv = cb_ref[...] + cw[CONV_WIDTH - 1:CONV_WIDTH, :] * u
    for back in range(1, CONV_WIDTH):
        k = CONV_WIDTH - 1 - back
        conv = conv + cw[k:k + 1, :] * ubuf[tail - back:tail - back + tc, :]
    ubuf[0:tail, :] = u[tc - tail:tc, :]

    conv_b = conv.astype(BF16)
    r = _sigmoid(_mm(conv_b, wa_ref[...]) + ba_ref[...])
    gate_i = _sigmoid(_mm(conv_b, wi_ref[...]) + bi_ref[...])
    log_a = (-LRU_C) * r * _softplus(-lam_ref[...])
    a = jnp.exp(log_a)
    x_in = jnp.sqrt(-jnp.tanh(log_a) * (a * a + 1.0)) * (gate_i * conv)
    h = _scan_linear(a, x_in, hprev[7:8, :])
    hprev[...] = h[tc - 8:tc, :]
    ya_ref[...] = (h * _gelu_tanh(gate_ref[...])).astype(BF16)

    z = f_ref[...] + bf_ref[...]
    log_f = jnp.minimum(z, 0.0) - jnp.log1p(jnp.exp(-jnp.abs(z)))
    c = _scan_sum(log_f, cprev[7:8, :])
    cprev[...] = c[tc - 8:tc, :]
    c_ref[...] = c


def _lru(u, gate, f, conv_w, conv_b, wa, ba, wi, bi, lam, bf, batch, seq):
    t = u.shape[0]
    tc = SCAN_TILE
    nt = seq // tc
    sds = jax.ShapeDtypeStruct

    def rows(width):
        return pl.BlockSpec((tc, width), lambda b, j: (b * nt + j, 0))

    small = [conv_w, conv_b, wa, ba, wi, bi, lam, bf]
    return pl.pallas_call(
        _lru_kernel,
        grid=(batch, nt),
        in_specs=[rows(LRU_WIDTH), rows(LRU_WIDTH), rows(LANES)] + [_resident(a.shape) for a in small],
        out_specs=[rows(LRU_WIDTH), rows(LANES)],
        out_shape=[sds((t, LRU_WIDTH), BF16), sds((t, LANES), F32)],
        scratch_shapes=[pltpu.VMEM((tc + 2 * CONV_WIDTH, LRU_WIDTH), F32),
                        pltpu.VMEM((8, LRU_WIDTH), F32), pltpu.VMEM((8, LANES), F32)],
        compiler_params=_params("parallel", "arbitrary"),
        name="rglru",
    )(u, gate, f, *small)


HEAD_PAIRS = N_HEADS // 2


def _pair(ref, rows, p):
    return ref[rows, p * LANES:(p + 1) * LANES]


def _low_head(shape):
    return lax.broadcasted_iota(jnp.int32, shape, 1) < HEAD_DIM


def _split_pair(x):
    low = _low_head(x.shape)
    zero = jnp.zeros_like(x)
    return jnp.where(low, x, zero), jnp.where(low, zero, x)


def _pair_values(v, ones_ref):
    v_lo, v_hi = _split_pair(v)
    return jnp.concatenate([jnp.concatenate([v_lo, ones_ref[0]], axis=1),
                            jnp.concatenate([v_hi, ones_ref[1]], axis=1)], axis=0)


def _head_ones(keys):
    low = (jnp.arange(LANES) < HEAD_DIM).astype(BF16)
    return jnp.broadcast_to(jnp.stack([low, 1 - low])[:, None, :], (2, keys, LANES))


def _attn_begin(q_ref, qm_s, m_s, acc_s):
    for p in range(HEAD_PAIRS):
        qm_s[2 * p], qm_s[2 * p + 1] = _split_pair(_pair(q_ref, slice(None), p))
    m_s[...] = jnp.full(m_s.shape, NEG_INF, F32)
    acc_s[...] = jnp.zeros_like(acc_s)


def _attn_block(qm_s, k_ref, v_ref, ones_ref, m_s, acc_s, rows, width, bias_fn):
    tq = qm_s.shape[1]
    low_q = _low_head((tq, LANES))
    for p in range(HEAD_PAIRS):
        kp = _pair(k_ref, rows, p)
        rhs = _pair_values(_pair(v_ref, rows, p), ones_ref)
        probs, drops = [], []
        for e in range(2):
            h = 2 * p + e
            s = _mm_nt(qm_s[h], kp) + bias_fn(h)
            m_prev = m_s[h]
            m_new = jnp.maximum(m_prev, jnp.max(s, axis=-1, keepdims=True))
            m_s[h] = m_new
            probs.append(jnp.exp2(s - jnp.concatenate([m_new] * (width // LANES), axis=1)).astype(BF16))
            drops.append(m_prev - m_new)
        alpha = jnp.exp2(jnp.where(low_q, drops[0], drops[1]))
        acc_s[p] = acc_s[p] * jnp.concatenate([alpha, alpha], axis=1) + _mm(jnp.concatenate(probs, axis=1), rhs)


def _attn_finish(o_ref, acc_s):
    for p in range(HEAD_PAIRS):
        acc = acc_s[p]
        o_ref[:, p * LANES:(p + 1) * LANES] = (acc[:, :LANES] / acc[:, LANES:]).astype(BF16)


def _attn_scratch(tq):
    return [pltpu.VMEM((N_HEADS, tq, LANES), BF16), pltpu.VMEM((N_HEADS, tq, LANES), F32),
            pltpu.VMEM((HEAD_PAIRS, tq, 2 * LANES), F32)]


def _kv_rows(j, tq):
    return pl.ds(pl.multiple_of(j * tq, tq), tq)


def _fox_kernel(q_ref, k_ref, v_ref, ones_ref, c_ref, o_ref, qm_s, m_s, acc_s):
    tq = q_ref.shape[0]
    i = pl.program_id(1)
    state = (qm_s, k_ref, v_ref, ones_ref, m_s, acc_s)
    _attn_begin(q_ref, qm_s, m_s, acc_s)

    def full_block(j, _):
        rows = _kv_rows(j, tq)
        _attn_block(*state, rows, tq, lambda h: c_ref[h:h + 1, rows] * (-LOG2E))
        return 0

    lax.fori_loop(0, i, full_block, 0)
    row = lax.broadcasted_iota(jnp.int32, (tq, tq), 0)
    col = lax.broadcasted_iota(jnp.int32, (tq, tq), 1)
    causal = jnp.where(row >= col, 0.0, NEG_INF)
    rows = _kv_rows(i, tq)
    _attn_block(*state, rows, tq, lambda h: causal + c_ref[h:h + 1, rows] * (-LOG2E))
    _attn_finish(o_ref, acc_s)


def _fox(q, k, v, c_rows, batch, seq):
    t = q.shape[0]
    tq = ATTN_TILE
    nq = seq // tq
    seq_block = pl.BlockSpec((None, seq, BRANCH_W), lambda b, i: (b, 0, 0))
    return pl.pallas_call(
        _fox_kernel,
        grid=(batch, nq),
        in_specs=[pl.BlockSpec((tq, BRANCH_W), lambda b, i: (b * nq + i, 0)), seq_block, seq_block,
                  _resident((2, tq, LANES)), pl.BlockSpec((None, N_HEADS, seq), lambda b, i: (b, 0, 0))],
        out_specs=pl.BlockSpec((tq, BRANCH_W), lambda b, i: (b * nq + i, 0)),
        out_shape=jax.ShapeDtypeStruct((t, BRANCH_W), BF16),
        scratch_shapes=_attn_scratch(tq),
        compiler_params=_params("arbitrary", "arbitrary"),
        name="fox_attention",
    )(q, k.reshape(batch, seq, BRANCH_W), v.reshape(batch, seq, BRANCH_W), _head_ones(tq), c_rows)


DIL_NEAR_BLOCKS = -(-DIL_MID_WINDOW // ATTN_TILE)


def _dilated_kernel(q_ref, k_ref, v_ref, ones_ref, o_ref, qm_s, m_s, acc_s, bias_s):
    tq = q_ref.shape[0]
    i = pl.program_id(1)
    far = DIL_NEAR_BLOCKS + 1

    @pl.when((pl.program_id(0) == 0) & (i == 0))
    def _():
        row = lax.broadcasted_iota(jnp.int32, (tq, tq), 0)
        col = lax.broadcasted_iota(jnp.int32, (tq, tq), 1)
        diff = row - col
        for delta in range(far):
            dist = diff + delta * tq
            ok = dist >= 0
            count = ((ok & (dist <= DIL_NEAR_WINDOW)).astype(F32)
                     + (ok & (dist <= DIL_MID_WINDOW) & (dist % DIL_MID_STEP == 0)).astype(F32)
                     + (ok & (dist % DIL_FAR_STEP == 0)).astype(F32))
            bias_s[delta] = jnp.log2(count)
        bias_s[far] = jnp.where(diff % DIL_FAR_STEP == 0, 0.0, NEG_INF)

    state = (qm_s, k_ref, v_ref, ones_ref, m_s, acc_s)
    _attn_begin(q_ref, qm_s, m_s, acc_s)
    _attn_block(*state, _kv_rows(i, tq), tq, lambda h: bias_s[0])
    for delta in range(1, far):
        @pl.when(i >= delta)
        def _(delta=delta):
            _attn_block(*state, _kv_rows(i - delta, tq), tq, lambda h: bias_s[delta])

    def far_block(j, _):
        _attn_block(*state, _kv_rows(j, tq), tq, lambda h: bias_s[far])
        return 0

    lax.fori_loop(0, jnp.maximum(i - DIL_NEAR_BLOCKS, 0), far_block, 0)
    _attn_finish(o_ref, acc_s)


def _dilated(q, k, v, batch, seq):
    t = q.shape[0]
    tq = ATTN_TILE
    nq = seq // tq
    seq_block = pl.BlockSpec((None, seq, BRANCH_W), lambda b, i: (b, 0, 0))
    return pl.pallas_call(
        _dilated_kernel,
        grid=(batch, nq),
        in_specs=[pl.BlockSpec((tq, BRANCH_W), lambda b, i: (b * nq + i, 0)), seq_block, seq_block,
                  _resident((2, tq, LANES))],
        out_specs=pl.BlockSpec((tq, BRANCH_W), lambda b, i: (b * nq + i, 0)),
        out_shape=jax.ShapeDtypeStruct((t, BRANCH_W), BF16),
        scratch_shapes=_attn_scratch(tq) + [pltpu.VMEM((DIL_NEAR_BLOCKS + 2, tq, tq), F32)],
        compiler_params=_params("arbitrary", "arbitrary"),
        name="dilated_attention",
    )(q, k.reshape(batch, seq, BRANCH_W), v.reshape(batch, seq, BRANCH_W), _head_ones(tq))


SWA_GROUP = N_HEADS // SWA_KV_HEADS
SWA_HEAD_ORDER = tuple(g * SWA_GROUP + p for p in range(SWA_GROUP) for g in range(SWA_KV_HEADS))


SWA_TILE = 512


def _swa_kernel(sink_ref, q_ref, kp_ref, kc_ref, vp_ref, vc_ref, ones_ref, o_ref):
    w = SWA_WINDOW
    i = pl.program_id(1)
    row = lax.broadcasted_iota(jnp.int32, (w, 2 * w), 0)
    col = lax.broadcasted_iota(jnp.int32, (w, 2 * w), 1)
    dist = row + w - col
    in_window = (dist >= 0) & (dist < SWA_WINDOW)
    bias = jnp.where(in_window, 0.0, NEG_INF)
    bias_first = jnp.where(in_window & ((col >= w) | (i > 0)), 0.0, NEG_INF)
    k_all = jnp.concatenate([kp_ref[...], kc_ref[...]], axis=0)
    v_all = jnp.concatenate([vp_ref[...], vc_ref[...]], axis=0)
    low_q = _low_head((w, LANES))
    for j in range(q_ref.shape[0] // w):
        k = k_all[j * w:(j + 2) * w]
        rhs = _pair_values(v_all[j * w:(j + 2) * w], ones_ref)
        for p in range(SWA_GROUP):
            probs, sink_drop = [], []
            for g, qm in enumerate(_split_pair(q_ref[j * w:(j + 1) * w, p * LANES:(p + 1) * LANES])):
                sink = sink_ref[g * SWA_GROUP + p] * LOG2E
                s = _mm_nt(qm, k) + (bias_first if j == 0 else bias)
                m = jnp.maximum(jnp.max(s, axis=-1, keepdims=True), sink)
                probs.append(jnp.exp2(s - m).astype(BF16))
                sink_drop.append(sink - m)
            out = _mm(jnp.concatenate(probs, axis=1), rhs)
            den = out[:, LANES:] + jnp.exp2(jnp.where(low_q, sink_drop[0], sink_drop[1]))
            o_ref[j * w:(j + 1) * w, p * LANES:(p + 1) * LANES] = (out[:, :LANES] / den).astype(BF16)


def _swa(q, k, v, sink, batch, seq):
    t = q.shape[0]
    tq = SWA_TILE
    nq = seq // tq
    sub = tq // SWA_WINDOW
    cur = pl.BlockSpec((tq, SWA_KW), lambda b, i: (b * nq + i, 0))
    prev = pl.BlockSpec((SWA_WINDOW, SWA_KW), lambda b, i: ((b * nq + i) * sub - jnp.minimum(i, 1), 0))
    return pl.pallas_call(
        _swa_kernel,
        grid=(batch, nq),
        in_specs=[pl.BlockSpec(memory_space=pltpu.SMEM),
                  pl.BlockSpec((tq, BRANCH_W), lambda b, i: (b * nq + i, 0)), prev, cur, prev, cur,
                  _resident((2, 2 * SWA_WINDOW, LANES))],
        out_specs=pl.BlockSpec((tq, BRANCH_W), lambda b, i: (b * nq + i, 0)),
        out_shape=jax.ShapeDtypeStruct((t, BRANCH_W), BF16),
        compiler_params=_params("parallel", "arbitrary"),
        name="swa_attention",
    )(sink, q, k, k, v, v, _head_ones(2 * SWA_WINDOW))


def _outproj_kernel(h_ref, ya_ref, yb_ref, wa_ref, wb_ref, o_ref):
    o_ref[...] = h_ref[...] + _mm(ya_ref[...], wa_ref[...]) + _mm(yb_ref[...], wb_ref[...])


def _outproj(h, ya, yb, wa, wb):
    t = h.shape[0]
    return pl.pallas_call(
        _outproj_kernel,
        grid=(t // ROW_TILE,),
        in_specs=[_rows(ROW_TILE, D_MODEL), _rows(ROW_TILE, BRANCH_W), _rows(ROW_TILE, BRANCH_W),
                  _resident(wa.shape), _resident(wb.shape)],
        out_specs=_rows(ROW_TILE, D_MODEL),
        out_shape=jax.ShapeDtypeStruct((t, D_MODEL), F32),
        compiler_params=_params("parallel"),
        name="mixer_outproj",
    )(h, ya, yb, wa, wb)


def _memkv_kernel(x_ref, g_ref, w_ref, o_ref):
    xn = _rmsnorm(x_ref[...], g_ref[...]).astype(BF16)
    o_ref[...] = _mm(xn, w_ref[...]).astype(BF16)


def _memkv(mem, g, w):
    t = mem.shape[0]
    return pl.pallas_call(
        _memkv_kernel,
        grid=(t // ROW_TILE,),
        in_specs=[_rows(ROW_TILE, D_MODEL), _resident(g.shape), _resident(w.shape)],
        out_specs=_rows(ROW_TILE, 2 * D_MODEL),
        out_shape=jax.ShapeDtypeStruct((t, 2 * D_MODEL), BF16),
        compiler_params=_params("parallel"),
        name="memory_kv",
    )(mem, g, w)


def _xattn_kernel(h_ref, g_ref, wq_ref, kv_ref, wo_ref, o_ref):
    h = h_ref[...]
    xn = _rmsnorm(h, g_ref[...]).astype(BF16)
    q = (_mm(xn, wq_ref[...]) * XA_SCALE).astype(BF16)
    outs = []
    for hd in range(XA_HEADS):
        lo = hd * XA_HEAD_DIM
        s = _mm_nt(q[:, lo:lo + XA_HEAD_DIM], kv_ref[:, lo:lo + XA_HEAD_DIM])
        p = jnp.exp(s - jnp.max(s, axis=-1, keepdims=True))
        l = jnp.sum(p, axis=-1, keepdims=True)
        v = kv_ref[:, D_MODEL + lo:D_MODEL + lo + XA_HEAD_DIM]
        outs.append((_mm(p.astype(BF16), v) / l).astype(BF16))
    o_ref[...] = h + _mm(jnp.concatenate(outs, axis=1), wo_ref[...])


def _xattn(h, g, wq, kv, wo, seq):
    t = h.shape[0]
    tiles_per_seq = seq // ROW_TILE
    return pl.pallas_call(
        _xattn_kernel,
        grid=(t // ROW_TILE,),
        in_specs=[_rows(ROW_TILE, D_MODEL), _resident(g.shape), _resident(wq.shape),
                  pl.BlockSpec((None, MEM_LEN, 2 * D_MODEL), lambda i: (i // tiles_per_seq, 0, 0)),
                  _resident(wo.shape)],
        out_specs=_rows(ROW_TILE, D_MODEL),
        out_shape=jax.ShapeDtypeStruct((t, D_MODEL), F32),
        compiler_params=_params("parallel"),
        name="memory_xattn",
    )(h, g, wq, kv, wo)


FF_CHUNK = 512


def _mlp_kernel(h_ref, g_ref, wup_ref, wdown_ref, gout_ref, o_ref, *, final_norm):
    h = h_ref[...]
    xn = _rmsnorm(h, g_ref[...]).astype(BF16)
    acc = h
    for c in range(D_FF // FF_CHUNK):
        cols = slice(c * FF_CHUNK, (c + 1) * FF_CHUNK)
        up = jnp.maximum(_mm(xn, wup_ref[:, cols]), 0.0)
        acc = acc + _mm((up * up).astype(BF16), wdown_ref[cols, :])
    if final_norm:
        acc = _rmsnorm(acc, gout_ref[...])
    o_ref[...] = acc


def _mlp(h, g, wup, wdown, gout, final_norm):
    t = h.shape[0]
    return pl.pallas_call(
        functools.partial(_mlp_kernel, final_norm=final_norm),
        grid=(t // ROW_TILE,),
        in_specs=[_rows(ROW_TILE, D_MODEL), _resident(g.shape), _resident(wup.shape), _resident(wdown.shape),
                  _resident(gout.shape)],
        out_specs=_rows(ROW_TILE, D_MODEL),
        out_shape=jax.ShapeDtypeStruct((t, D_MODEL), F32),
        compiler_params=_params("parallel"),
        name="relu2_mlp",
    )(h, g, wup, wdown, gout)


def _block_diag(w):
    g, c, d = w.shape
    eye = jnp.eye(g, dtype=w.dtype)
    return (w[:, :, None, :] * eye[:, None, :, None]).reshape(g * c, g * d)


def _rope_tables(seq):
    half = HEAD_DIM // 2
    inv = ROPE_THETA ** (-jnp.arange(half, dtype=F32) / half)
    ang = jnp.arange(seq, dtype=F32)[:, None] * inv[None, :]
    reps = LANES // half
    cos = jnp.tile(jnp.cos(ang), (1, reps))
    sign = jnp.tile(jnp.concatenate([-jnp.ones((half,), F32), jnp.ones((half,), F32)]), LANES // HEAD_DIM)
    sin = jnp.tile(jnp.sin(ang), (1, reps)) * sign[None, :]
    return cos, sin


def _permute_heads(w, axis):
    shape = w.shape
    split = shape[:axis] + (N_HEADS, HEAD_DIM) + shape[axis + 1:]
    return jnp.take(w.reshape(split), jnp.array(SWA_HEAD_ORDER), axis=axis).reshape(shape)


def _row(v, width=None):
    v = v.astype(F32).reshape(1, -1)
    if width is not None and v.shape[1] < width:
        v = jnp.pad(v, ((0, 0), (0, width - v.shape[1])))
    return v


def kernel(x, mem, ab_norm, ab_w_in, ab_conv_w, ab_conv_b, lru_w_a, lru_b_a, lru_w_i, lru_b_i, lru_lambda, fox_b_f, ab_w_out, cd_norm, cd_w_in, cd_sink, cd_w_out, xa_norm, xa_mem_norm, xa_w_q, xa_w_kv, xa_w_o, mlp_norm, mlp_w_up, mlp_w_down, final_norm):
    batch, seq, d = x.shape
    assert d == D_MODEL and seq % ROW_TILE == 0 and seq % ATTN_TILE == 0 and seq % SWA_TILE == 0 and mem.shape[1] == MEM_LEN
    depth = xa_norm.shape[0]
    h = x.reshape(batch * seq, d)
    mem_rows = mem.reshape(batch * MEM_LEN, d)
    cos, sin = _rope_tables(seq)
    main_w = 5 * BRANCH_W

    for layer in range(depth):
        j = layer // 2
        if layer % 2 == 0:
            w_in = ab_w_in[j]
            w_f = jnp.pad(w_in[:, main_w:], ((0, 0), (0, LANES - N_HEADS))).astype(BF16)
            u, gate, q, k, v, f = _inproj_ab(h, _row(ab_norm[j]), w_in[:, :main_w].astype(BF16), w_f)
            ya, c = _lru(u, gate, f, ab_conv_w[j].astype(F32), _row(ab_conv_b[j]),
                         _block_diag(lru_w_a[j]).astype(BF16), _row(lru_b_a[j]),
                         _block_diag(lru_w_i[j]).astype(BF16), _row(lru_b_i[j]),
                         _row(lru_lambda[j]), _row(fox_b_f[j], LANES), batch, seq)
            c_rows = c.reshape(batch, seq, LANES)[:, :, :N_HEADS].transpose(0, 2, 1)
            yb = _fox(q, k, v, c_rows, batch, seq)
            w_out = ab_w_out[j].astype(BF16)
        else:
            w_in = cd_w_in[j]
            qd_lo, qd_hi = 3 * BRANCH_W, 4 * BRANCH_W
            w_in = jnp.concatenate([w_in[:, :qd_lo], _permute_heads(w_in[:, qd_lo:qd_hi], 1), w_in[:, qd_hi:]], axis=1)
            qc, kc, vc, qd, kd, vd = _inproj_cd(h, _row(cd_norm[j]), w_in.astype(BF16), cos, sin, seq)
            ya = _dilated(qc, kc, vc, batch, seq)
            yb = _swa(qd, kd, vd, cd_sink[j].astype(F32), batch, seq)
            w_out = cd_w_out[j]
            w_out = jnp.concatenate([w_out[:BRANCH_W], _permute_heads(w_out[BRANCH_W:], 0)], axis=0).astype(BF16)
        h = _outproj(h, ya, yb, w_out[:BRANCH_W], w_out[BRANCH_W:])
        kv = _memkv(mem_rows, _row(xa_mem_norm[layer]), xa_w_kv[layer].astype(BF16))
        h = _xattn(h, _row(xa_norm[layer]), xa_w_q[layer].astype(BF16),
                   kv.reshape(batch, MEM_LEN, 2 * D_MODEL), xa_w_o[layer].astype(BF16), seq)
        h = _mlp(h, _row(mlp_norm[layer]), mlp_w_up[layer].astype(BF16), mlp_w_down[layer].astype(BF16),
                 _row(final_norm), final_norm=(layer == depth - 1))
    return h.reshape(batch, seq, d)
```

```python
import functools
import math

import jax
import jax.numpy as jnp
from jax import lax
from jax.experimental import pallas as pl
from jax.experimental.pallas import tpu as pltpu

F32 = jnp.float32
BF16 = jnp.bfloat16

D_MODEL = 1024
HEAD_DIM = 64
N_HEADS = 8
BRANCH_W = N_HEADS * HEAD_DIM
LRU_WIDTH = 512
LRU_BLOCKS = 8
CONV_WIDTH = 4
LRU_C = 8.0
SWA_KV_HEADS = 2
SWA_KW = SWA_KV_HEADS * HEAD_DIM
SWA_WINDOW = 128
DIL_NEAR_WINDOW = 128
DIL_MID_WINDOW = 512
DIL_MID_STEP = 4
DIL_FAR_STEP = 16
MEM_LEN = 256
XA_HEADS = 4
XA_HEAD_DIM = D_MODEL // XA_HEADS
D_FF = 4 * D_MODEL
ROPE_THETA = 10000.0
EPS = 1e-6
LOG2E = math.log2(math.e)
ATTN_SCALE = HEAD_DIM ** -0.5
Q_SCALE_LOG2 = ATTN_SCALE * LOG2E
XA_SCALE = XA_HEAD_DIM ** -0.5
LANES = 128
NEG_INF = float("-inf")

ROW_TILE = 512
SCAN_TILE = 256
ATTN_TILE = 512
VMEM_LIMIT = 48 * 1024 * 1024


def _params(*sem):
    return pltpu.CompilerParams(dimension_semantics=sem, vmem_limit_bytes=VMEM_LIMIT)


def _resident(shape):
    zeros = (0,) * len(shape)
    return pl.BlockSpec(shape, lambda *_: zeros, pipeline_mode=pl.Buffered(1))


def _rows(tile, width):
    return pl.BlockSpec((tile, width), lambda i: (i, 0))


def _rmsnorm(x, g):
    return x * lax.rsqrt(jnp.mean(x * x, axis=-1, keepdims=True) + EPS) * g


def _mm(a, b):
    return jnp.dot(a, b, preferred_element_type=F32)


def _mm_nt(a, b):
    return lax.dot_general(a, b, (((1,), (1,)), ((), ())), preferred_element_type=F32)


def _inproj_ab_kernel(x_ref, g_ref, w_ref, wf_ref, u_ref, gate_ref, q_ref, k_ref, v_ref, f_ref):
    xn = _rmsnorm(x_ref[...], g_ref[...]).astype(BF16)
    w = BRANCH_W
    u_ref[...] = _mm(xn, w_ref[:, 0:w])
    gate_ref[...] = _mm(xn, w_ref[:, w:2 * w])
    q_ref[...] = (_mm(xn, w_ref[:, 2 * w:3 * w]) * Q_SCALE_LOG2).astype(BF16)
    k_ref[...] = _mm(xn, w_ref[:, 3 * w:4 * w]).astype(BF16)
    v_ref[...] = _mm(xn, w_ref[:, 4 * w:5 * w]).astype(BF16)
    f_ref[...] = _mm(xn, wf_ref[...])


def _inproj_ab(h, g, w_main, w_f):
    t = h.shape[0]
    sds = jax.ShapeDtypeStruct
    return pl.pallas_call(
        _inproj_ab_kernel,
        grid=(t // ROW_TILE,),
        in_specs=[_rows(ROW_TILE, D_MODEL), _resident(g.shape), _resident(w_main.shape), _resident(w_f.shape)],
        out_specs=[_rows(ROW_TILE, BRANCH_W)] * 5 + [_rows(ROW_TILE, LANES)],
        out_shape=[sds((t, BRANCH_W), F32), sds((t, BRANCH_W), F32), sds((t, BRANCH_W), BF16),
                   sds((t, BRANCH_W), BF16), sds((t, BRANCH_W), BF16), sds((t, LANES), F32)],
        compiler_params=_params("parallel"),
        name="inproj_ab",
    )(h, g, w_main, w_f)


def _inproj_cd_kernel(x_ref, g_ref, w_ref, cos_ref, sin_ref, qc_ref, kc_ref, vc_ref, qd_ref, kd_ref, vd_ref):
    xn = _rmsnorm(x_ref[...], g_ref[...]).astype(BF16)
    cos = cos_ref[...]
    sin = sin_ref[...]
    lane = lax.broadcasted_iota(jnp.int32, cos.shape, 1)
    first_half = (lane % HEAD_DIM) < (HEAD_DIM // 2)

    def rope(z, scale):
        outs = []
        for c in range(z.shape[1] // LANES):
            zc = z[:, c * LANES:(c + 1) * LANES]
            partner = jnp.where(first_half, pltpu.roll(zc, LANES - HEAD_DIM // 2, 1), pltpu.roll(zc, HEAD_DIM // 2, 1))
            outs.append(((zc * cos + partner * sin) * scale).astype(BF16))
        return outs[0] if len(outs) == 1 else jnp.concatenate(outs, axis=1)

    w = BRANCH_W
    qc_ref[...] = rope(_mm(xn, w_ref[:, 0:w]), Q_SCALE_LOG2)
    kc_ref[...] = rope(_mm(xn, w_ref[:, w:2 * w]), 1.0)
    vc_ref[...] = _mm(xn, w_ref[:, 2 * w:3 * w]).astype(BF16)
    qd_ref[...] = rope(_mm(xn, w_ref[:, 3 * w:4 * w]), Q_SCALE_LOG2)
    kd_ref[...] = rope(_mm(xn, w_ref[:, 4 * w:4 * w + SWA_KW]), 1.0)
    vd_ref[...] = _mm(xn, w_ref[:, 4 * w + SWA_KW:4 * w + 2 * SWA_KW]).astype(BF16)


def _inproj_cd(h, g, w, cos, sin, seq):
    t = h.shape[0]
    sds = jax.ShapeDtypeStruct
    tiles_per_seq = seq // ROW_TILE
    table = pl.BlockSpec((ROW_TILE, LANES), lambda i: (i % tiles_per_seq, 0))
    return pl.pallas_call(
        _inproj_cd_kernel,
        grid=(t // ROW_TILE,),
        in_specs=[_rows(ROW_TILE, D_MODEL), _resident(g.shape), _resident(w.shape), table, table],
        out_specs=[_rows(ROW_TILE, BRANCH_W)] * 4 + [_rows(ROW_TILE, SWA_KW)] * 2,
        out_shape=[sds((t, BRANCH_W), BF16)] * 4 + [sds((t, SWA_KW), BF16)] * 2,
        compiler_params=_params("parallel"),
        name="inproj_cd",
    )(h, g, w, cos, sin)


SUBLANES = 8


def _group_rows(x):
    return x.reshape(x.shape[0] // SUBLANES, SUBLANES, x.shape[1])


def _scan_linear(a, x, h0):
    a3, x3 = _group_rows(a), _group_rows(x)
    row = lax.broadcasted_iota(jnp.int32, a3.shape, 1)
    d = 1
    while d < SUBLANES:
        keep = row >= d
        x3 = jnp.where(keep, x3 + a3 * pltpu.roll(x3, d, 1), x3)
        a3 = jnp.where(keep, a3 * pltpu.roll(a3, d, 1), a3)
        d *= 2
    out, carry = [], h0
    for g in range(a3.shape[0]):
        hg = a3[g] * carry + x3[g]
        out.append(hg)
        carry = hg[SUBLANES - 1:SUBLANES, :]
    return jnp.concatenate(out, axis=0)


def _scan_sum(x, c0):
    x3 = _group_rows(x)
    row = lax.broadcasted_iota(jnp.int32, x3.shape, 1)
    d = 1
    while d < SUBLANES:
        x3 = jnp.where(row >= d, x3 + pltpu.roll(x3, d, 1), x3)
        d *= 2
    out, carry = [], c0
    for g in range(x3.shape[0]):
        cg = x3[g] + carry
        out.append(cg)
        carry = cg[SUBLANES - 1:SUBLANES, :]
    return jnp.concatenate(out, axis=0)


def _softplus(x):
    return jnp.maximum(x, 0.0) + jnp.log1p(jnp.exp(-jnp.abs(x)))


def _sigmoid(x):
    return 1.0 / (1.0 + jnp.exp(-x))


def _gelu_tanh(x):
    return 0.5 * x * (1.0 + jnp.tanh(math.sqrt(2.0 / math.pi) * (x + 0.044715 * (x * x * x))))


def _lru_kernel(u_ref, gate_ref, f_ref, cw_ref, cb_ref, wa_ref, ba_ref, wi_ref, bi_ref, lam_ref, bf_ref,
                ya_ref, c_ref, ubuf, hprev, cprev):
    tc = u_ref.shape[0]
    tail = CONV_WIDTH * 2

    @pl.when(pl.program_id(1) == 0)
    def _():
        ubuf[0:tail, :] = jnp.zeros((tail, LRU_WIDTH), F32)
        hprev[...] = jnp.zeros_like(hprev)
        cprev[...] = jnp.zeros_like(cprev)

    u = u_ref[...]
    ubuf[tail:tail + tc, :] = u
    cw = cw_ref[...]
    conv = cb_ref[...] + cw[CONV_WIDTH - 1:CONV_WIDTH, :] * u
    for back in range(1, CONV_WIDTH):
        k = CONV_WIDTH - 1 - back
        conv = conv + cw[k:k + 1, :] * ubuf[tail - back:tail - back + tc, :]
    ubuf[0:tail, :] = u[tc - tail:tc, :]

    conv_b = conv.astype(BF16)
    r = _sigmoid(_mm(conv_b, wa_ref[...]) + ba_ref[...])
    gate_i = _sigmoid(_mm(conv_b, wi_ref[...]) + bi_ref[...])
    log_a = (-LRU_C) * r * _softplus(-lam_ref[...])
    a = jnp.exp(log_a)
    x_in = jnp.sqrt(-jnp.tanh(log_a) * (a * a + 1.0)) * (gate_i * conv)
    h = _scan_linear(a, x_in, hprev[7:8, :])
    hprev[...] = h[tc - 8:tc, :]
    ya_ref[...] = (h * _gelu_tanh(gate_ref[...])).astype(BF16)

    z = f_ref[...] + bf_ref[...]
    log_f = jnp.minimum(z, 0.0) - jnp.log1p(jnp.exp(-jnp.abs(z)))
    c = _scan_sum(log_f, cprev[7:8, :])
    cprev[...] = c[tc - 8:tc, :]
    c_ref[...] = c


def _lru(u, gate, f, conv_w, conv_b, wa, ba, wi, bi, lam, bf, batch, seq):
    t = u.shape[0]
    tc = SCAN_TILE
    nt = seq // tc
    sds = jax.ShapeDtypeStruct

    def rows(width):
        return pl.BlockSpec((tc, width), lambda b, j: (b * nt + j, 0))

    small = [conv_w, conv_b, wa, ba, wi, bi, lam, bf]
    return pl.pallas_call(
        _lru_kernel,
        grid=(batch, nt),
        in_specs=[rows(LRU_WIDTH), rows(LRU_WIDTH), rows(LANES)] + [_resident(a.shape) for a in small],
        out_specs=[rows(LRU_WIDTH), rows(LANES)],
        out_shape=[sds((t, LRU_WIDTH), BF16), sds((t, LANES), F32)],
        scratch_shapes=[pltpu.VMEM((tc + 2 * CONV_WIDTH, LRU_WIDTH), F32),
                        pltpu.VMEM((8, LRU_WIDTH), F32), pltpu.VMEM((8, LANES), F32)],
        compiler_params=_params("parallel", "arbitrary"),
        name="rglru",
    )(u, gate, f, *small)


HEAD_PAIRS = N_HEADS // 2


def _pair(ref, rows, p):
    return ref[rows, p * LANES:(p + 1) * LANES]


def _low_head(shape):
    return lax.broadcasted_iota(jnp.int32, shape, 1) < HEAD_DIM


def _split_pair(x):
    low = _low_head(x.shape)
    zero = jnp.zeros_like(x)
    return jnp.where(low, x, zero), jnp.where(low, zero, x)


def _pair_values(v, ones_ref):
    keys = v.shape[0]
    v_lo, v_hi = _split_pair(v)
    return jnp.concatenate([jnp.concatenate([v_lo, ones_ref[0, :keys]], axis=1),
                            jnp.concatenate([v_hi, ones_ref[1, :keys]], axis=1)], axis=0)


def _head_ones(keys):
    low = (jnp.arange(LANES) < HEAD_DIM).astype(BF16)
    return jnp.broadcast_to(jnp.stack([low, 1 - low])[:, None, :], (2, keys, LANES))


def _attn_begin(q_ref, qm_s):
    for p in range(HEAD_PAIRS):
        qm_s[2 * p], qm_s[2 * p + 1] = _split_pair(_pair(q_ref, slice(None), p))


def _attn_block(qm_s, k_ref, v_ref, ones_ref, m_s, acc_s, start, width, bias_fn, q_rows=slice(None), first=False):
    rows = pl.ds(start, width)
    nq = len(range(*q_rows.indices(qm_s.shape[1])))
    low_q = _low_head((nq, LANES))
    for p in range(HEAD_PAIRS):
        kp = _pair(k_ref, rows, p)
        rhs = _pair_values(_pair(v_ref, rows, p), ones_ref)
        probs, drops = [], []
        for e in range(2):
            h = 2 * p + e
            s = _mm_nt(qm_s[h, q_rows], kp) + bias_fn(h)
            m_new = jnp.max(s, axis=-1, keepdims=True)
            if first:
                m_new = jnp.broadcast_to(m_new, (nq, LANES))
            else:
                m_prev = m_s[h, q_rows]
                m_new = jnp.maximum(m_prev, m_new)
                drops.append(m_prev - m_new)
            m_s[h, q_rows] = m_new
            probs.append(jnp.exp2(s - jnp.concatenate([m_new] * (width // LANES), axis=1)).astype(BF16))
        update = _mm(jnp.concatenate(probs, axis=1), rhs)
        if not first:
            alpha = jnp.exp2(jnp.where(low_q, drops[0], drops[1]))
            update = acc_s[p, q_rows] * jnp.concatenate([alpha, alpha], axis=1) + update
        acc_s[p, q_rows] = update


def _attn_diagonal(state, i, tq, bias_fn):
    half = tq // 2
    first_keys = pl.multiple_of(i * tq, tq)
    second_keys = pl.multiple_of(i * tq + half, half)
    _attn_block(*state, first_keys, half, lambda h: bias_fn(h, pl.ds(first_keys, half), 0, 0), first=True)
    _attn_block(*state, second_keys, half, lambda h: bias_fn(h, pl.ds(second_keys, half), half, half),
                q_rows=slice(half, tq))


def _attn_finish(o_ref, acc_s):
    for p in range(HEAD_PAIRS):
        acc = acc_s[p]
        o_ref[:, p * LANES:(p + 1) * LANES] = (acc[:, :LANES] / acc[:, LANES:]).astype(BF16)


def _attn_scratch(tq):
    return [pltpu.VMEM((N_HEADS, tq, LANES), BF16), pltpu.VMEM((N_HEADS, tq, LANES), F32),
            pltpu.VMEM((HEAD_PAIRS, tq, 2 * LANES), F32)]


def _fox_kernel(q_ref, k_ref, v_ref, ones_ref, c_ref, o_ref, qm_s, m_s, acc_s):
    tq = q_ref.shape[0]
    half = tq // 2
    i = pl.program_id(1)
    state = (qm_s, k_ref, v_ref, ones_ref, m_s, acc_s)
    _attn_begin(q_ref, qm_s)
    row = lax.broadcasted_iota(jnp.int32, (tq, half), 0)
    col = lax.broadcasted_iota(jnp.int32, (tq, half), 1)
    causal = jnp.where(row >= col, 0.0, NEG_INF)

    def gate_bias(h, key_rows):
        return c_ref[h:h + 1, key_rows] * (-LOG2E)

    _attn_diagonal(state, i, tq, lambda h, key_rows, q_lo, k_lo: (causal if q_lo == k_lo == 0 else causal[:half])
                   + gate_bias(h, key_rows))

    def full_block(j, _):
        start = pl.multiple_of(j * tq, tq)
        _attn_block(*state, start, tq, lambda h: gate_bias(h, pl.ds(start, tq)))
        return 0

    lax.fori_loop(0, i, full_block, 0)
    _attn_finish(o_ref, acc_s)


def _fox(q, k, v, c_rows, batch, seq):
    t = q.shape[0]
    tq = ATTN_TILE
    nq = seq // tq
    seq_block = pl.BlockSpec((None, seq, BRANCH_W), lambda b, i: (b, 0, 0))
    return pl.pallas_call(
        _fox_kernel,
        grid=(batch, nq),
        in_specs=[pl.BlockSpec((tq, BRANCH_W), lambda b, i: (b * nq + i, 0)), seq_block, seq_block,
                  _resident((2, tq, LANES)), pl.BlockSpec((None, N_HEADS, seq), lambda b, i: (b, 0, 0))],
        out_specs=pl.BlockSpec((tq, BRANCH_W), lambda b, i: (b * nq + i, 0)),
        out_shape=jax.ShapeDtypeStruct((t, BRANCH_W), BF16),
        scratch_shapes=_attn_scratch(tq),
        compiler_params=_params("arbitrary", "arbitrary"),
        name="fox_attention",
    )(q, k.reshape(batch, seq, BRANCH_W), v.reshape(batch, seq, BRANCH_W), _head_ones(tq), c_rows)


DIL_NEAR_BLOCKS = -(-DIL_MID_WINDOW // ATTN_TILE)


def _dilated_kernel(q_ref, k_ref, v_ref, ones_ref, o_ref, qm_s, m_s, acc_s, bias_s):
    tq = q_ref.shape[0]
    i = pl.program_id(1)
    far = DIL_NEAR_BLOCKS + 1

    @pl.when((pl.program_id(0) == 0) & (i == 0))
    def _():
        row = lax.broadcasted_iota(jnp.int32, (tq, tq), 0)
        col = lax.broadcasted_iota(jnp.int32, (tq, tq), 1)
        diff = row - col
        for delta in range(far):
            dist = diff + delta * tq
            ok = dist >= 0
            count = ((ok & (dist <= DIL_NEAR_WINDOW)).astype(F32)
                     + (ok & (dist <= DIL_MID_WINDOW) & (dist % DIL_MID_STEP == 0)).astype(F32)
                     + (ok & (dist % DIL_FAR_STEP == 0)).astype(F32))
            bias_s[delta] = jnp.log2(count)
        bias_s[far] = jnp.where(diff % DIL_FAR_STEP == 0, 0.0, NEG_INF)

    state = (qm_s, k_ref, v_ref, ones_ref, m_s, acc_s)
    _attn_begin(q_ref, qm_s)
    _attn_diagonal(state, i, tq, lambda h, key_rows, q_lo, k_lo: bias_s[0, q_lo:, k_lo:k_lo + tq // 2])
    for delta in range(1, far):
        @pl.when(i >= delta)
        def _(delta=delta):
            _attn_block(*state, pl.multiple_of((i - delta) * tq, tq), tq, lambda h: bias_s[delta])

    def far_block(j, _):
        _attn_block(*state, pl.multiple_of(j * tq, tq), tq, lambda h: bias_s[far])
        return 0

    lax.fori_loop(0, jnp.maximum(i - DIL_NEAR_BLOCKS, 0), far_block, 0)
    _attn_finish(o_ref, acc_s)


def _dilated(q, k, v, batch, seq):
    t = q.shape[0]
    tq = ATTN_TILE
    nq = seq // tq
    seq_block = pl.BlockSpec((None, seq, BRANCH_W), lambda b, i: (b, 0, 0))
    return pl.pallas_call(
        _dilated_kernel,
        grid=(batch, nq),
        in_specs=[pl.BlockSpec((tq, BRANCH_W), lambda b, i: (b * nq + i, 0)), seq_block, seq_block,
                  _resident((2, tq, LANES))],
        out_specs=pl.BlockSpec((tq, BRANCH_W), lambda b, i: (b * nq + i, 0)),
        out_shape=jax.ShapeDtypeStruct((t, BRANCH_W), BF16),
        scratch_shapes=_attn_scratch(tq) + [pltpu.VMEM((DIL_NEAR_BLOCKS + 2, tq, tq), F32)],
        compiler_params=_params("arbitrary", "arbitrary"),
        name="dilated_attention",
    )(q, k.reshape(batch, seq, BRANCH_W), v.reshape(batch, seq, BRANCH_W), _head_ones(tq))


SWA_GROUP = N_HEADS // SWA_KV_HEADS
SWA_HEAD_ORDER = tuple(g * SWA_GROUP + p for p in range(SWA_GROUP) for g in range(SWA_KV_HEADS))


SWA_TILE = 512


def _swa_kernel(sink_ref, q_ref, kp_ref, kc_ref, vp_ref, vc_ref, ones_ref, o_ref):
    w = SWA_WINDOW
    i = pl.program_id(1)
    row = lax.broadcasted_iota(jnp.int32, (w, 2 * w), 0)
    col = lax.broadcasted_iota(jnp.int32, (w, 2 * w), 1)
    dist = row + w - col
    in_window = (dist >= 0) & (dist < SWA_WINDOW)
    bias = jnp.where(in_window, 0.0, NEG_INF)
    bias_first = jnp.where(in_window & ((col >= w) | (i > 0)), 0.0, NEG_INF)
    k_all = jnp.concatenate([kp_ref[...], kc_ref[...]], axis=0)
    v_all = jnp.concatenate([vp_ref[...], vc_ref[...]], axis=0)
    low_q = _low_head((w, LANES))
    for j in range(q_ref.shape[0] // w):
        k = k_all[j * w:(j + 2) * w]
        rhs = _pair_values(v_all[j * w:(j + 2) * w], ones_ref)
        for p in range(SWA_GROUP):
            probs, sink_drop = [], []
            for g, qm in enumerate(_split_pair(q_ref[j * w:(j + 1) * w, p * LANES:(p + 1) * LANES])):
                sink = sink_ref[g * SWA_GROUP + p] * LOG2E
                s = _mm_nt(qm, k) + (bias_first if j == 0 else bias)
                m = jnp.maximum(jnp.max(s, axis=-1, keepdims=True), sink)
                probs.append(jnp.exp2(s - m).astype(BF16))
                sink_drop.append(sink - m)
            out = _mm(jnp.concatenate(probs, axis=1), rhs)
            den = out[:, LANES:] + jnp.exp2(jnp.where(low_q, sink_drop[0], sink_drop[1]))
            o_ref[j * w:(j + 1) * w, p * LANES:(p + 1) * LANES] = (out[:, :LANES] / den).astype(BF16)


def _swa(q, k, v, sink, batch, seq):
    t = q.shape[0]
    tq = SWA_TILE
    nq = seq // tq
    sub = tq // SWA_WINDOW
    cur = pl.BlockSpec((tq, SWA_KW), lambda b, i: (b * nq + i, 0))
    prev = pl.BlockSpec((SWA_WINDOW, SWA_KW), lambda b, i: ((b * nq + i) * sub - jnp.minimum(i, 1), 0))
    return pl.pallas_call(
        _swa_kernel,
        grid=(batch, nq),
        in_specs=[pl.BlockSpec(memory_space=pltpu.SMEM),
                  pl.BlockSpec((tq, BRANCH_W), lambda b, i: (b * nq + i, 0)), prev, cur, prev, cur,
                  _resident((2, 2 * SWA_WINDOW, LANES))],
        out_specs=pl.BlockSpec((tq, BRANCH_W), lambda b, i: (b * nq + i, 0)),
        out_shape=jax.ShapeDtypeStruct((t, BRANCH_W), BF16),
        compiler_params=_params("parallel", "arbitrary"),
        name="swa_attention",
    )(sink, q, k, k, v, v, _head_ones(2 * SWA_WINDOW))


def _memkv_kernel(x_ref, g_ref, w_ref, o_ref):
    xn = _rmsnorm(x_ref[...], g_ref[...]).astype(BF16)
    o_ref[...] = _mm(xn, w_ref[...]).astype(BF16)


def _memkv(mem, g, w):
    t = mem.shape[0]
    return pl.pallas_call(
        _memkv_kernel,
        grid=(t // ROW_TILE,),
        in_specs=[_rows(ROW_TILE, D_MODEL), _resident(g.shape), _resident(w.shape)],
        out_specs=_rows(ROW_TILE, 2 * D_MODEL),
        out_shape=jax.ShapeDtypeStruct((t, 2 * D_MODEL), BF16),
        compiler_params=_params("parallel"),
        name="memory_kv",
    )(mem, g, w)


def _xattn_kernel(h_ref, ya_ref, yb_ref, wa_ref, wb_ref, g_ref, wq_ref, kv_ref, wo_ref, o_ref):
    h = h_ref[...] + _mm(ya_ref[...], wa_ref[...]) + _mm(yb_ref[...], wb_ref[...])
    xn = _rmsnorm(h, g_ref[...]).astype(BF16)
    q = (_mm(xn, wq_ref[...]) * XA_SCALE).astype(BF16)
    outs = []
    for hd in range(XA_HEADS):
        lo = hd * XA_HEAD_DIM
        s = _mm_nt(q[:, lo:lo + XA_HEAD_DIM], kv_ref[:, lo:lo + XA_HEAD_DIM])
        p = jnp.exp(s - jnp.max(s, axis=-1, keepdims=True))
        l = jnp.sum(p, axis=-1, keepdims=True)
        v = kv_ref[:, D_MODEL + lo:D_MODEL + lo + XA_HEAD_DIM]
        outs.append((_mm(p.astype(BF16), v) / l).astype(BF16))
    o_ref[...] = h + _mm(jnp.concatenate(outs, axis=1), wo_ref[...])


def _xattn(h, ya, yb, wa, wb, g, wq, kv, wo, seq):
    t = h.shape[0]
    tiles_per_seq = seq // ROW_TILE
    return pl.pallas_call(
        _xattn_kernel,
        grid=(t // ROW_TILE,),
        in_specs=[_rows(ROW_TILE, D_MODEL), _rows(ROW_TILE, BRANCH_W), _rows(ROW_TILE, BRANCH_W),
                  _resident(wa.shape), _resident(wb.shape), _resident(g.shape), _resident(wq.shape),
                  pl.BlockSpec((None, MEM_LEN, 2 * D_MODEL), lambda i: (i // tiles_per_seq, 0, 0)),
                  _resident(wo.shape)],
        out_specs=_rows(ROW_TILE, D_MODEL),
        out_shape=jax.ShapeDtypeStruct((t, D_MODEL), F32),
        compiler_params=_params("parallel"),
        name="memory_xattn",
    )(h, ya, yb, wa, wb, g, wq, kv, wo)


FF_CHUNK = 512


def _mlp_kernel(h_ref, g_ref, wup_ref, wdown_ref, gout_ref, o_ref, *, final_norm):
    h = h_ref[...]
    xn = _rmsnorm(h, g_ref[...]).astype(BF16)
    acc = h
    for c in range(D_FF // FF_CHUNK):
        cols = slice(c * FF_CHUNK, (c + 1) * FF_CHUNK)
        up = jnp.maximum(_mm(xn, wup_ref[:, cols]), 0.0)
        acc = acc + _mm((up * up).astype(BF16), wdown_ref[cols, :])
    if final_norm:
        acc = _rmsnorm(acc, gout_ref[...])
    o_ref[...] = acc


def _mlp(h, g, wup, wdown, gout, final_norm):
    t = h.shape[0]
    return pl.pallas_call(
        functools.partial(_mlp_kernel, final_norm=final_norm),
        grid=(t // ROW_TILE,),
        in_specs=[_rows(ROW_TILE, D_MODEL), _resident(g.shape), _resident(wup.shape), _resident(wdown.shape),
                  _resident(gout.shape)],
        out_specs=_rows(ROW_TILE, D_MODEL),
        out_shape=jax.ShapeDtypeStruct((t, D_MODEL), F32),
        compiler_params=_params("parallel"),
        name="relu2_mlp",
    )(h, g, wup, wdown, gout)


def _block_diag(w):
    g, c, d = w.shape
    eye = jnp.eye(g, dtype=w.dtype)
    return (w[:, :, None, :] * eye[:, None, :, None]).reshape(g * c, g * d)


def _rope_tables(seq):
    half = HEAD_DIM // 2
    inv = ROPE_THETA ** (-jnp.arange(half, dtype=F32) / half)
    ang = jnp.arange(seq, dtype=F32)[:, None] * inv[None, :]
    reps = LANES // half
    cos = jnp.tile(jnp.cos(ang), (1, reps))
    sign = jnp.tile(jnp.concatenate([-jnp.ones((half,), F32), jnp.ones((half,), F32)]), LANES // HEAD_DIM)
    sin = jnp.tile(jnp.sin(ang), (1, reps)) * sign[None, :]
    return cos, sin


def _permute_heads(w, axis):
    shape = w.shape
    split = shape[:axis] + (N_HEADS, HEAD_DIM) + shape[axis + 1:]
    return jnp.take(w.reshape(split), jnp.array(SWA_HEAD_ORDER), axis=axis).reshape(shape)


def _row(v, width=None):
    v = v.astype(F32).reshape(1, -1)
    if width is not None and v.shape[1] < width:
        v = jnp.pad(v, ((0, 0), (0, width - v.shape[1])))
    return v


def kernel(x, mem, ab_norm, ab_w_in, ab_conv_w, ab_conv_b, lru_w_a, lru_b_a, lru_w_i, lru_b_i, lru_lambda, fox_b_f, ab_w_out, cd_norm, cd_w_in, cd_sink, cd_w_out, xa_norm, xa_mem_norm, xa_w_q, xa_w_kv, xa_w_o, mlp_norm, mlp_w_up, mlp_w_down, final_norm):
    batch, seq, d = x.shape
    assert d == D_MODEL and seq % ROW_TILE == 0 and seq % ATTN_TILE == 0 and seq % SWA_TILE == 0 and mem.shape[1] == MEM_LEN
    depth = xa_norm.shape[0]
    h = x.reshape(batch * seq, d)
    mem_rows = mem.reshape(batch * MEM_LEN, d)
    cos, sin = _rope_tables(seq)
    main_w = 5 * BRANCH_W

    for layer in range(depth):
        j = layer // 2
        if layer % 2 == 0:
            w_in = ab_w_in[j]
            w_f = jnp.pad(w_in[:, main_w:], ((0, 0), (0, LANES - N_HEADS))).astype(BF16)
            u, gate, q, k, v, f = _inproj_ab(h, _row(ab_norm[j]), w_in[:, :main_w].astype(BF16), w_f)
            ya, c = _lru(u, gate, f, ab_conv_w[j].astype(F32), _row(ab_conv_b[j]),
                         _block_diag(lru_w_a[j]).astype(BF16), _row(lru_b_a[j]),
                         _block_diag(lru_w_i[j]).astype(BF16), _row(lru_b_i[j]),
                         _row(lru_lambda[j]), _row(fox_b_f[j], LANES), batch, seq)
            c_rows = c.reshape(batch, seq, LANES)[:, :, :N_HEADS].transpose(0, 2, 1)
            yb = _fox(q, k, v, c_rows, batch, seq)
            w_out = ab_w_out[j].astype(BF16)
        else:
            w_in = cd_w_in[j]
            qd_lo, qd_hi = 3 * BRANCH_W, 4 * BRANCH_W
            w_in = jnp.concatenate([w_in[:, :qd_lo], _permute_heads(w_in[:, qd_lo:qd_hi], 1), w_in[:, qd_hi:]], axis=1)
            qc, kc, vc, qd, kd, vd = _inproj_cd(h, _row(cd_norm[j]), w_in.astype(BF16), cos, sin, seq)
            ya = _dilated(qc, kc, vc, batch, seq)
            yb = _swa(qd, kd, vd, cd_sink[j].astype(F32), batch, seq)
            w_out = cd_w_out[j]
            w_out = jnp.concatenate([w_out[:BRANCH_W], _permute_heads(w_out[BRANCH_W:], 0)], axis=0).astype(BF16)
        kv = _memkv(mem_rows, _row(xa_mem_norm[layer]), xa_w_kv[layer].astype(BF16))
        h = _xattn(h, ya, yb, w_out[:BRANCH_W], w_out[BRANCH_W:], _row(xa_norm[layer]), xa_w_q[layer].astype(BF16),
                   kv.reshape(batch, MEM_LEN, 2 * D_MODEL), xa_w_o[layer].astype(BF16), seq)
        h = _mlp(h, _row(mlp_norm[layer]), mlp_w_up[layer].astype(BF16), mlp_w_down[layer].astype(BF16),
                 _row(final_norm), final_norm=(layer == depth - 1))
    return h.reshape(batch, seq, d)
```

```python
import functools
import math

import jax
import jax.numpy as jnp
from jax import lax
from jax.experimental import pallas as pl
from jax.experimental.pallas import tpu as pltpu

F32 = jnp.float32
BF16 = jnp.bfloat16

D_MODEL = 1024
HEAD_DIM = 64
N_HEADS = 8
BRANCH_W = N_HEADS * HEAD_DIM
LRU_WIDTH = 512
LRU_BLOCKS = 8
CONV_WIDTH = 4
LRU_C = 8.0
SWA_KV_HEADS = 2
SWA_KW = SWA_KV_HEADS * HEAD_DIM
SWA_WINDOW = 128
DIL_NEAR_WINDOW = 128
DIL_MID_WINDOW = 512
DIL_MID_STEP = 4
DIL_FAR_STEP = 16
MEM_LEN = 256
XA_HEADS = 4
XA_HEAD_DIM = D_MODEL // XA_HEADS
D_FF = 4 * D_MODEL
ROPE_THETA = 10000.0
EPS = 1e-6
LOG2E = math.log2(math.e)
ATTN_SCALE = HEAD_DIM ** -0.5
Q_SCALE_LOG2 = ATTN_SCALE * LOG2E
XA_SCALE = XA_HEAD_DIM ** -0.5
LANES = 128
NEG_INF = float("-inf")

ROW_TILE = 512
SUB_TILE = 512
MLP_TILE = 1024
ATTN_TILE = 512
VMEM_LIMIT = 48 * 1024 * 1024


def _params(*sem):
    return pltpu.CompilerParams(dimension_semantics=sem, vmem_limit_bytes=VMEM_LIMIT)


def _resident(shape):
    zeros = (0,) * len(shape)
    return pl.BlockSpec(shape, lambda *_: zeros, pipeline_mode=pl.Buffered(1))


def _rows(tile, width):
    return pl.BlockSpec((tile, width), lambda i: (i, 0))


def _rmsnorm(x, g):
    return x * lax.rsqrt(jnp.mean(x * x, axis=-1, keepdims=True) + EPS) * g


def _mm(a, b):
    return jnp.dot(a, b, preferred_element_type=F32)


def _mm_nt(a, b):
    return lax.dot_general(a, b, (((1,), (1,)), ((), ())), preferred_element_type=F32)


def _inproj_cd_kernel(x_ref, g_ref, w_ref, cos_ref, sin_ref, qc_ref, kc_ref, vc_ref, qd_ref, kd_ref, vd_ref):
    xn = _rmsnorm(x_ref[...], g_ref[...]).astype(BF16)
    cos = cos_ref[...]
    sin = sin_ref[...]
    lane = lax.broadcasted_iota(jnp.int32, cos.shape, 1)
    first_half = (lane % HEAD_DIM) < (HEAD_DIM // 2)

    def rope(z, scale):
        outs = []
        for c in range(z.shape[1] // LANES):
            zc = z[:, c * LANES:(c + 1) * LANES]
            partner = jnp.where(first_half, pltpu.roll(zc, LANES - HEAD_DIM // 2, 1), pltpu.roll(zc, HEAD_DIM // 2, 1))
            outs.append(((zc * cos + partner * sin) * scale).astype(BF16))
        return outs[0] if len(outs) == 1 else jnp.concatenate(outs, axis=1)

    w = BRANCH_W
    qc_ref[...] = rope(_mm(xn, w_ref[:, 0:w]), Q_SCALE_LOG2)
    kc_ref[...] = rope(_mm(xn, w_ref[:, w:2 * w]), 1.0)
    vc_ref[...] = _mm(xn, w_ref[:, 2 * w:3 * w]).astype(BF16)
    qd_ref[...] = rope(_mm(xn, w_ref[:, 3 * w:4 * w]), Q_SCALE_LOG2)
    kd_ref[...] = rope(_mm(xn, w_ref[:, 4 * w:4 * w + SWA_KW]), 1.0)
    vd_ref[...] = _mm(xn, w_ref[:, 4 * w + SWA_KW:4 * w + 2 * SWA_KW]).astype(BF16)


def _inproj_cd(h, g, w, cos, sin, seq):
    t = h.shape[0]
    sds = jax.ShapeDtypeStruct
    tiles_per_seq = seq // ROW_TILE
    table = pl.BlockSpec((ROW_TILE, LANES), lambda i: (i % tiles_per_seq, 0))
    return pl.pallas_call(
        _inproj_cd_kernel,
        grid=(t // ROW_TILE,),
        in_specs=[_rows(ROW_TILE, D_MODEL), _resident(g.shape), _resident(w.shape), table, table],
        out_specs=[_rows(ROW_TILE, BRANCH_W)] * 4 + [_rows(ROW_TILE, SWA_KW)] * 2,
        out_shape=[sds((t, BRANCH_W), BF16)] * 4 + [sds((t, SWA_KW), BF16)] * 2,
        compiler_params=_params("parallel"),
        name="inproj_cd",
    )(h, g, w, cos, sin)


SUBLANES = 8


def _group_rows(x):
    return x.reshape(x.shape[0] // SUBLANES, SUBLANES, x.shape[1])


def _scan_linear(a, x, h0):
    a3, x3 = _group_rows(a), _group_rows(x)
    row = lax.broadcasted_iota(jnp.int32, a3.shape, 1)
    d = 1
    while d < SUBLANES:
        keep = row >= d
        x3 = jnp.where(keep, x3 + a3 * pltpu.roll(x3, d, 1), x3)
        a3 = jnp.where(keep, a3 * pltpu.roll(a3, d, 1), a3)
        d *= 2
    out, carry = [], h0
    for g in range(a3.shape[0]):
        hg = a3[g] * carry + x3[g]
        out.append(hg)
        carry = hg[SUBLANES - 1:SUBLANES, :]
    return jnp.concatenate(out, axis=0)


def _scan_sum(x, c0):
    x3 = _group_rows(x)
    row = lax.broadcasted_iota(jnp.int32, x3.shape, 1)
    d = 1
    while d < SUBLANES:
        x3 = jnp.where(row >= d, x3 + pltpu.roll(x3, d, 1), x3)
        d *= 2
    out, carry = [], c0
    for g in range(x3.shape[0]):
        cg = x3[g] + carry
        out.append(cg)
        carry = cg[SUBLANES - 1:SUBLANES, :]
    return jnp.concatenate(out, axis=0)


def _softplus(x):
    return jnp.maximum(x, 0.0) + jnp.log1p(jnp.exp(-jnp.abs(x)))


def _sigmoid(x):
    return 1.0 / (1.0 + jnp.exp(-x))


def _gelu_tanh(x):
    return 0.5 * x * (1.0 + jnp.tanh(math.sqrt(2.0 / math.pi) * (x + 0.044715 * (x * x * x))))


CONV_TAIL = CONV_WIDTH * 2


def _reset_recurrence(ubuf, hprev, cprev):
    ubuf[0:CONV_TAIL, :] = jnp.zeros((CONV_TAIL, LRU_WIDTH), F32)
    hprev[...] = jnp.zeros_like(hprev)
    cprev[...] = jnp.zeros_like(cprev)


def _recurrent_branch(u, gate, f, cw_ref, cb_ref, wa_ref, ba_ref, wi_ref, bi_ref, lam_ref, bf_ref,
                      ubuf, hprev, cprev):
    tc = u.shape[0]
    tail = CONV_TAIL
    ubuf[tail:tail + tc, :] = u
    cw = cw_ref[...]
    conv = cb_ref[...] + cw[CONV_WIDTH - 1:CONV_WIDTH, :] * u
    for back in range(1, CONV_WIDTH):
        k = CONV_WIDTH - 1 - back
        conv = conv + cw[k:k + 1, :] * ubuf[tail - back:tail - back + tc, :]
    ubuf[0:tail, :] = u[tc - tail:tc, :]

    conv_b = conv.astype(BF16)
    r = _sigmoid(_mm(conv_b, wa_ref[...]) + ba_ref[...])
    gate_i = _sigmoid(_mm(conv_b, wi_ref[...]) + bi_ref[...])
    log_a = (-LRU_C) * r * _softplus(-lam_ref[...])
    a = jnp.exp(log_a)
    x_in = jnp.sqrt(-jnp.tanh(log_a) * (a * a + 1.0)) * (gate_i * conv)
    h = _scan_linear(a, x_in, hprev[7:8, :])
    hprev[...] = h[tc - 8:tc, :]
    ya = h * _gelu_tanh(gate)

    z = f + bf_ref[...]
    log_f = jnp.minimum(z, 0.0) - jnp.log1p(jnp.exp(-jnp.abs(z)))
    c = _scan_sum(log_f, cprev[7:8, :])
    cprev[...] = c[tc - 8:tc, :]
    return ya, c


def _inproj_ab_kernel(x_ref, g_ref, w_ref, wf_ref, cw_ref, cb_ref, wa_ref, ba_ref, wi_ref, bi_ref, lam_ref, bf_ref,
                      ya_ref, q_ref, k_ref, v_ref, c_ref, ubuf, hprev, cprev, *, tiles_per_seq):
    pl.when(pl.program_id(0) % tiles_per_seq == 0)(lambda: _reset_recurrence(ubuf, hprev, cprev))
    xn = _rmsnorm(x_ref[...], g_ref[...]).astype(BF16)
    w = BRANCH_W
    u = _mm(xn, w_ref[:, 0:w])
    gate = _mm(xn, w_ref[:, w:2 * w])
    f = _mm(xn, wf_ref[...])
    q_ref[...] = (_mm(xn, w_ref[:, 2 * w:3 * w]) * Q_SCALE_LOG2).astype(BF16)
    k_ref[...] = _mm(xn, w_ref[:, 3 * w:4 * w]).astype(BF16)
    v_ref[...] = _mm(xn, w_ref[:, 4 * w:5 * w]).astype(BF16)
    ya, c = _recurrent_branch(u, gate, f, cw_ref, cb_ref, wa_ref, ba_ref, wi_ref, bi_ref, lam_ref, bf_ref,
                              ubuf, hprev, cprev)
    ya_ref[...] = ya.astype(BF16)
    c_ref[...] = c


def _inproj_ab(h, g, w_main, w_f, conv_w, conv_b, wa, ba, wi, bi, lam, bf, seq):
    t = h.shape[0]
    sds = jax.ShapeDtypeStruct
    small = [g, w_main, w_f, conv_w, conv_b, wa, ba, wi, bi, lam, bf]
    return pl.pallas_call(
        functools.partial(_inproj_ab_kernel, tiles_per_seq=seq // ROW_TILE),
        grid=(t // ROW_TILE,),
        in_specs=[_rows(ROW_TILE, D_MODEL)] + [_resident(a.shape) for a in small],
        out_specs=[_rows(ROW_TILE, BRANCH_W)] * 4 + [_rows(ROW_TILE, LANES)],
        out_shape=[sds((t, BRANCH_W), BF16)] * 4 + [sds((t, LANES), F32)],
        scratch_shapes=[pltpu.VMEM((ROW_TILE + CONV_TAIL, LRU_WIDTH), F32),
                        pltpu.VMEM((8, LRU_WIDTH), F32), pltpu.VMEM((8, LANES), F32)],
        compiler_params=_params("arbitrary"),
        name="inproj_ab_rglru",
    )(h, *small)


HEAD_PAIRS = N_HEADS // 2


def _pair(ref, rows, p):
    return ref[rows, p * LANES:(p + 1) * LANES]


def _low_head(shape):
    return lax.broadcasted_iota(jnp.int32, shape, 1) < HEAD_DIM


def _split_pair(x):
    low = _low_head(x.shape)
    zero = jnp.zeros_like(x)
    return jnp.where(low, x, zero), jnp.where(low, zero, x)


def _pair_values(v, ones_ref):
    keys = v.shape[0]
    v_lo, v_hi = _split_pair(v)
    return jnp.concatenate([jnp.concatenate([v_lo, ones_ref[0, :keys]], axis=1),
                            jnp.concatenate([v_hi, ones_ref[1, :keys]], axis=1)], axis=0)


def _head_ones(keys):
    low = (jnp.arange(LANES) < HEAD_DIM).astype(BF16)
    return jnp.broadcast_to(jnp.stack([low, 1 - low])[:, None, :], (2, keys, LANES))


def _attn_begin(q_ref, qm_s):
    for p in range(HEAD_PAIRS):
        qm_s[2 * p], qm_s[2 * p + 1] = _split_pair(_pair(q_ref, slice(None), p))


def _attn_block(qm_s, k_ref, v_ref, ones_ref, m_s, acc_s, start, width, bias_fn, q_rows=slice(None), first=False):
    rows = pl.ds(start, width)
    nq = len(range(*q_rows.indices(qm_s.shape[1])))
    low_q = _low_head((nq, LANES))
    for p in range(HEAD_PAIRS):
        kp = _pair(k_ref, rows, p)
        rhs = _pair_values(_pair(v_ref, rows, p), ones_ref)
        probs, drops = [], []
        for e in range(2):
            h = 2 * p + e
            s = _mm_nt(qm_s[h, q_rows], kp) + bias_fn(h)
            m_new = jnp.max(s, axis=-1, keepdims=True)
            if first:
                m_new = jnp.broadcast_to(m_new, (nq, LANES))
            else:
                m_prev = m_s[h, q_rows]
                m_new = jnp.maximum(m_prev, m_new)
                drops.append(m_prev - m_new)
            m_s[h, q_rows] = m_new
            probs.append(jnp.exp2(s - jnp.concatenate([m_new] * (width // LANES), axis=1)).astype(BF16))
        update = _mm(jnp.concatenate(probs, axis=1), rhs)
        if not first:
            alpha = jnp.exp2(jnp.where(low_q, drops[0], drops[1]))
            update = acc_s[p, q_rows] * jnp.concatenate([alpha, alpha], axis=1) + update
        acc_s[p, q_rows] = update


def _attn_diagonal(state, i, tq, bias_fn):
    half = tq // 2
    first_keys = pl.multiple_of(i * tq, tq)
    second_keys = pl.multiple_of(i * tq + half, half)
    _attn_block(*state, first_keys, half, lambda h: bias_fn(h, pl.ds(first_keys, half), 0, 0), first=True)
    _attn_block(*state, second_keys, half, lambda h: bias_fn(h, pl.ds(second_keys, half), half, half),
                q_rows=slice(half, tq))


def _attn_finish(o_ref, acc_s):
    for p in range(HEAD_PAIRS):
        acc = acc_s[p]
        o_ref[:, p * LANES:(p + 1) * LANES] = (acc[:, :LANES] / acc[:, LANES:]).astype(BF16)


def _attn_scratch(tq):
    return [pltpu.VMEM((N_HEADS, tq, LANES), BF16), pltpu.VMEM((N_HEADS, tq, LANES), F32),
            pltpu.VMEM((HEAD_PAIRS, tq, 2 * LANES), F32)]


def _fox_kernel(q_ref, k_ref, v_ref, ones_ref, c_ref, o_ref, qm_s, m_s, acc_s):
    tq = q_ref.shape[0]
    half = tq // 2
    i = pl.program_id(1)
    state = (qm_s, k_ref, v_ref, ones_ref, m_s, acc_s)
    _attn_begin(q_ref, qm_s)
    row = lax.broadcasted_iota(jnp.int32, (tq, half), 0)
    col = lax.broadcasted_iota(jnp.int32, (tq, half), 1)
    causal = jnp.where(row >= col, 0.0, NEG_INF)

    def gate_bias(h, key_rows):
        return c_ref[h:h + 1, key_rows] * (-LOG2E)

    _attn_diagonal(state, i, tq, lambda h, key_rows, q_lo, k_lo: (causal if q_lo == k_lo == 0 else causal[:half])
                   + gate_bias(h, key_rows))

    def full_block(j, _):
        start = pl.multiple_of(j * tq, tq)
        _attn_block(*state, start, tq, lambda h: gate_bias(h, pl.ds(start, tq)))
        return 0

    lax.fori_loop(0, i, full_block, 0)
    _attn_finish(o_ref, acc_s)


def _fox(q, k, v, c_rows, batch, seq):
    t = q.shape[0]
    tq = ATTN_TILE
    nq = seq // tq
    seq_block = pl.BlockSpec((None, seq, BRANCH_W), lambda b, i: (b, 0, 0))
    return pl.pallas_call(
        _fox_kernel,
        grid=(batch, nq),
        in_specs=[pl.BlockSpec((tq, BRANCH_W), lambda b, i: (b * nq + i, 0)), seq_block, seq_block,
                  _resident((2, tq, LANES)), pl.BlockSpec((None, N_HEADS, seq), lambda b, i: (b, 0, 0))],
        out_specs=pl.BlockSpec((tq, BRANCH_W), lambda b, i: (b * nq + i, 0)),
        out_shape=jax.ShapeDtypeStruct((t, BRANCH_W), BF16),
        scratch_shapes=_attn_scratch(tq),
        compiler_params=_params("arbitrary", "arbitrary"),
        name="fox_attention",
    )(q, k.reshape(batch, seq, BRANCH_W), v.reshape(batch, seq, BRANCH_W), _head_ones(tq), c_rows)


DIL_NEAR_BLOCKS = -(-DIL_MID_WINDOW // ATTN_TILE)


def _dilated_kernel(q_ref, k_ref, v_ref, ones_ref, o_ref, qm_s, m_s, acc_s, bias_s):
    tq = q_ref.shape[0]
    i = pl.program_id(1)
    far = DIL_NEAR_BLOCKS + 1

    @pl.when((pl.program_id(0) == 0) & (i == 0))
    def _():
        row = lax.broadcasted_iota(jnp.int32, (tq, tq), 0)
        col = lax.broadcasted_iota(jnp.int32, (tq, tq), 1)
        diff = row - col
        for delta in range(far):
            dist = diff + delta * tq
            ok = dist >= 0
            count = ((ok & (dist <= DIL_NEAR_WINDOW)).astype(F32)
                     + (ok & (dist <= DIL_MID_WINDOW) & (dist % DIL_MID_STEP == 0)).astype(F32)
                     + (ok & (dist % DIL_FAR_STEP == 0)).astype(F32))
            bias_s[delta] = jnp.log2(count)
        bias_s[far] = jnp.where(diff % DIL_FAR_STEP == 0, 0.0, NEG_INF)

    state = (qm_s, k_ref, v_ref, ones_ref, m_s, acc_s)
    _attn_begin(q_ref, qm_s)
    _attn_diagonal(state, i, tq, lambda h, key_rows, q_lo, k_lo: bias_s[0, q_lo:, k_lo:k_lo + tq // 2])
    for delta in range(1, far):
        @pl.when(i >= delta)
        def _(delta=delta):
            _attn_block(*state, pl.multiple_of((i - delta) * tq, tq), tq, lambda h: bias_s[delta])

    def far_block(j, _):
        _attn_block(*state, pl.multiple_of(j * tq, tq), tq, lambda h: bias_s[far])
        return 0

    lax.fori_loop(0, jnp.maximum(i - DIL_NEAR_BLOCKS, 0), far_block, 0)
    _attn_finish(o_ref, acc_s)


def _dilated(q, k, v, batch, seq):
    t = q.shape[0]
    tq = ATTN_TILE
    nq = seq // tq
    seq_block = pl.BlockSpec((None, seq, BRANCH_W), lambda b, i: (b, 0, 0))
    return pl.pallas_call(
        _dilated_kernel,
        grid=(batch, nq),
        in_specs=[pl.BlockSpec((tq, BRANCH_W), lambda b, i: (b * nq + i, 0)), seq_block, seq_block,
                  _resident((2, tq, LANES))],
        out_specs=pl.BlockSpec((tq, BRANCH_W), lambda b, i: (b * nq + i, 0)),
        out_shape=jax.ShapeDtypeStruct((t, BRANCH_W), BF16),
        scratch_shapes=_attn_scratch(tq) + [pltpu.VMEM((DIL_NEAR_BLOCKS + 2, tq, tq), F32)],
        compiler_params=_params("arbitrary", "arbitrary"),
        name="dilated_attention",
    )(q, k.reshape(batch, seq, BRANCH_W), v.reshape(batch, seq, BRANCH_W), _head_ones(tq))


SWA_GROUP = N_HEADS // SWA_KV_HEADS
SWA_HEAD_ORDER = tuple(g * SWA_GROUP + p for p in range(SWA_GROUP) for g in range(SWA_KV_HEADS))


SWA_TILE = 512


def _swa_kernel(sink_ref, q_ref, kp_ref, kc_ref, vp_ref, vc_ref, ones_ref, o_ref):
    w = SWA_WINDOW
    i = pl.program_id(1)
    row = lax.broadcasted_iota(jnp.int32, (w, 2 * w), 0)
    col = lax.broadcasted_iota(jnp.int32, (w, 2 * w), 1)
    dist = row + w - col
    in_window = (dist >= 0) & (dist < SWA_WINDOW)
    bias = jnp.where(in_window, 0.0, NEG_INF)
    bias_first = jnp.where(in_window & ((col >= w) | (i > 0)), 0.0, NEG_INF)
    k_all = jnp.concatenate([kp_ref[...], kc_ref[...]], axis=0)
    v_all = jnp.concatenate([vp_ref[...], vc_ref[...]], axis=0)
    low_q = _low_head((w, LANES))
    for j in range(q_ref.shape[0] // w):
        k = k_all[j * w:(j + 2) * w]
        rhs = _pair_values(v_all[j * w:(j + 2) * w], ones_ref)
        for p in range(SWA_GROUP):
            probs, sink_drop = [], []
            for g, qm in enumerate(_split_pair(q_ref[j * w:(j + 1) * w, p * LANES:(p + 1) * LANES])):
                sink = sink_ref[g * SWA_GROUP + p] * LOG2E
                s = _mm_nt(qm, k) + (bias_first if j == 0 else bias)
                m = jnp.maximum(jnp.max(s, axis=-1, keepdims=True), sink)
                probs.append(jnp.exp2(s - m).astype(BF16))
                sink_drop.append(sink - m)
            out = _mm(jnp.concatenate(probs, axis=1), rhs)
            den = out[:, LANES:] + jnp.exp2(jnp.where(low_q, sink_drop[0], sink_drop[1]))
            o_ref[j * w:(j + 1) * w, p * LANES:(p + 1) * LANES] = (out[:, :LANES] / den).astype(BF16)


def _swa(q, k, v, sink, batch, seq):
    t = q.shape[0]
    tq = SWA_TILE
    nq = seq // tq
    sub = tq // SWA_WINDOW
    cur = pl.BlockSpec((tq, SWA_KW), lambda b, i: (b * nq + i, 0))
    prev = pl.BlockSpec((SWA_WINDOW, SWA_KW), lambda b, i: ((b * nq + i) * sub - jnp.minimum(i, 1), 0))
    return pl.pallas_call(
        _swa_kernel,
        grid=(batch, nq),
        in_specs=[pl.BlockSpec(memory_space=pltpu.SMEM),
                  pl.BlockSpec((tq, BRANCH_W), lambda b, i: (b * nq + i, 0)), prev, cur, prev, cur,
                  _resident((2, 2 * SWA_WINDOW, LANES))],
        out_specs=pl.BlockSpec((tq, BRANCH_W), lambda b, i: (b * nq + i, 0)),
        out_shape=jax.ShapeDtypeStruct((t, BRANCH_W), BF16),
        compiler_params=_params("parallel", "arbitrary"),
        name="swa_attention",
    )(sink, q, k, k, v, v, _head_ones(2 * SWA_WINDOW))


def _memkv_kernel(x_ref, g_ref, w_ref, o_ref):
    xn = _rmsnorm(x_ref[...], g_ref[...]).astype(BF16)
    o_ref[...] = _mm(xn, w_ref[...]).astype(BF16)


def _memkv(mem, g, w):
    t = mem.shape[0]
    return pl.pallas_call(
        _memkv_kernel,
        grid=(t // ROW_TILE,),
        in_specs=[_rows(ROW_TILE, D_MODEL), _resident(g.shape), _resident(w.shape)],
        out_specs=_rows(ROW_TILE, 2 * D_MODEL),
        out_shape=jax.ShapeDtypeStruct((t, 2 * D_MODEL), BF16),
        compiler_params=_params("parallel"),
        name="memory_kv",
    )(mem, g, w)


def _xattn_kernel(h_ref, ya_ref, yb_ref, wa_ref, wb_ref, g_ref, wq_ref, kv_ref, wo_ref, o_ref):
    for r in range(h_ref.shape[0] // SUB_TILE):
        rows = slice(r * SUB_TILE, (r + 1) * SUB_TILE)
        h = h_ref[rows, :] + _mm(ya_ref[rows, :], wa_ref[...]) + _mm(yb_ref[rows, :], wb_ref[...])
        xn = _rmsnorm(h, g_ref[...]).astype(BF16)
        q = (_mm(xn, wq_ref[...]) * XA_SCALE).astype(BF16)
        outs = []
        for hd in range(XA_HEADS):
            lo = hd * XA_HEAD_DIM
            s = _mm_nt(q[:, lo:lo + XA_HEAD_DIM], kv_ref[:, lo:lo + XA_HEAD_DIM])
            p = jnp.exp(s - jnp.max(s, axis=-1, keepdims=True))
            l = jnp.sum(p, axis=-1, keepdims=True)
            v = kv_ref[:, D_MODEL + lo:D_MODEL + lo + XA_HEAD_DIM]
            outs.append((_mm(p.astype(BF16), v) / l).astype(BF16))
        o_ref[rows, :] = h + _mm(jnp.concatenate(outs, axis=1), wo_ref[...])


def _xattn(h, ya, yb, wa, wb, g, wq, kv, wo, seq):
    t = h.shape[0]
    tiles_per_seq = seq // MLP_TILE
    return pl.pallas_call(
        _xattn_kernel,
        grid=(t // MLP_TILE,),
        in_specs=[_rows(MLP_TILE, D_MODEL), _rows(MLP_TILE, BRANCH_W), _rows(MLP_TILE, BRANCH_W),
                  _resident(wa.shape), _resident(wb.shape), _resident(g.shape), _resident(wq.shape),
                  pl.BlockSpec((None, MEM_LEN, 2 * D_MODEL), lambda i: (i // tiles_per_seq, 0, 0)),
                  _resident(wo.shape)],
        out_specs=_rows(MLP_TILE, D_MODEL),
        out_shape=jax.ShapeDtypeStruct((t, D_MODEL), F32),
        compiler_params=_params("parallel"),
        name="memory_xattn",
    )(h, ya, yb, wa, wb, g, wq, kv, wo)


FF_CHUNK = 512


def _mlp_kernel(h_ref, g_ref, wup_ref, wdown_ref, gout_ref, o_ref, *, final_norm):
    for r in range(h_ref.shape[0] // SUB_TILE):
        rows = slice(r * SUB_TILE, (r + 1) * SUB_TILE)
        h = h_ref[rows, :]
        xn = _rmsnorm(h, g_ref[...]).astype(BF16)
        acc = h
        for c in range(D_FF // FF_CHUNK):
            cols = slice(c * FF_CHUNK, (c + 1) * FF_CHUNK)
            up = jnp.maximum(_mm(xn, wup_ref[:, cols]), 0.0)
            acc = acc + _mm((up * up).astype(BF16), wdown_ref[cols, :])
        if final_norm:
            acc = _rmsnorm(acc, gout_ref[...])
        o_ref[rows, :] = acc


def _mlp(h, g, wup, wdown, gout, final_norm):
    t = h.shape[0]
    return pl.pallas_call(
        functools.partial(_mlp_kernel, final_norm=final_norm),
        grid=(t // MLP_TILE,),
        in_specs=[_rows(MLP_TILE, D_MODEL), _resident(g.shape), _resident(wup.shape), _resident(wdown.shape),
                  _resident(gout.shape)],
        out_specs=_rows(MLP_TILE, D_MODEL),
        out_shape=jax.ShapeDtypeStruct((t, D_MODEL), F32),
        compiler_params=_params("parallel"),
        name="relu2_mlp",
    )(h, g, wup, wdown, gout)


def _block_diag(w):
    g, c, d = w.shape
    eye = jnp.eye(g, dtype=w.dtype)
    return (w[:, :, None, :] * eye[:, None, :, None]).reshape(g * c, g * d)


def _rope_tables(seq):
    half = HEAD_DIM // 2
    inv = ROPE_THETA ** (-jnp.arange(half, dtype=F32) / half)
    ang = jnp.arange(seq, dtype=F32)[:, None] * inv[None, :]
    reps = LANES // half
    cos = jnp.tile(jnp.cos(ang), (1, reps))
    sign = jnp.tile(jnp.concatenate([-jnp.ones((half,), F32), jnp.ones((half,), F32)]), LANES // HEAD_DIM)
    sin = jnp.tile(jnp.sin(ang), (1, reps)) * sign[None, :]
    return cos, sin


def _permute_heads(w, axis):
    shape = w.shape
    split = shape[:axis] + (N_HEADS, HEAD_DIM) + shape[axis + 1:]
    return jnp.take(w.reshape(split), jnp.array(SWA_HEAD_ORDER), axis=axis).reshape(shape)


def _row(v, width=None):
    v = v.astype(F32).reshape(1, -1)
    if width is not None and v.shape[1] < width:
        v = jnp.pad(v, ((0, 0), (0, width - v.shape[1])))
    return v


def kernel(x, mem, ab_norm, ab_w_in, ab_conv_w, ab_conv_b, lru_w_a, lru_b_a, lru_w_i, lru_b_i, lru_lambda, fox_b_f, ab_w_out, cd_norm, cd_w_in, cd_sink, cd_w_out, xa_norm, xa_mem_norm, xa_w_q, xa_w_kv, xa_w_o, mlp_norm, mlp_w_up, mlp_w_down, final_norm):
    batch, seq, d = x.shape
    assert d == D_MODEL and seq % ROW_TILE == 0 and seq % ATTN_TILE == 0 and seq % SWA_TILE == 0 and mem.shape[1] == MEM_LEN
    depth = xa_norm.shape[0]
    h = x.reshape(batch * seq, d)
    mem_rows = mem.reshape(batch * MEM_LEN, d)
    cos, sin = _rope_tables(seq)
    main_w = 5 * BRANCH_W

    for layer in range(depth):
        j = layer // 2
        if layer % 2 == 0:
            w_in = ab_w_in[j]
            w_f = jnp.pad(w_in[:, main_w:], ((0, 0), (0, LANES - N_HEADS))).astype(BF16)
            ya, q, k, v, c = _inproj_ab(h, _row(ab_norm[j]), w_in[:, :main_w].astype(BF16), w_f,
                                        ab_conv_w[j].astype(F32), _row(ab_conv_b[j]),
                                        _block_diag(lru_w_a[j]).astype(BF16), _row(lru_b_a[j]),
                                        _block_diag(lru_w_i[j]).astype(BF16), _row(lru_b_i[j]),
                                        _row(lru_lambda[j]), _row(fox_b_f[j], LANES), seq)
            c_rows = c.reshape(batch, seq, LANES)[:, :, :N_HEADS].transpose(0, 2, 1)
            yb = _fox(q, k, v, c_rows, batch, seq)
            w_out = ab_w_out[j].astype(BF16)
        else:
            w_in = cd_w_in[j]
            qd_lo, qd_hi = 3 * BRANCH_W, 4 * BRANCH_W
            w_in = jnp.concatenate([w_in[:, :qd_lo], _permute_heads(w_in[:, qd_lo:qd_hi], 1), w_in[:, qd_hi:]], axis=1)
            qc, kc, vc, qd, kd, vd = _inproj_cd(h, _row(cd_norm[j]), w_in.astype(BF16), cos, sin, seq)
            ya = _dilated(qc, kc, vc, batch, seq)
            yb = _swa(qd, kd, vd, cd_sink[j].astype(F32), batch, seq)
            w_out = cd_w_out[j]
            w_out = jnp.concatenate([w_out[:BRANCH_W], _permute_heads(w_out[BRANCH_W:], 0)], axis=0).astype(BF16)
        kv = _memkv(mem_rows, _row(xa_mem_norm[layer]), xa_w_kv[layer].astype(BF16))
        h = _xattn(h, ya, yb, w_out[:BRANCH_W], w_out[BRANCH_W:], _row(xa_norm[layer]), xa_w_q[layer].astype(BF16),
                   kv.reshape(batch, MEM_LEN, 2 * D_MODEL), xa_w_o[layer].astype(BF16), seq)
        h = _mlp(h, _row(mlp_norm[layer]), mlp_w_up[layer].astype(BF16), mlp_w_down[layer].astype(BF16),
                 _row(final_norm), final_norm=(layer == depth - 1))
    return h.reshape(batch, seq, d)
```

```python
import functools
import math

import jax
import jax.numpy as jnp
from jax import lax
from jax.experimental import pallas as pl
from jax.experimental.pallas import tpu as pltpu

F32 = jnp.float32
BF16 = jnp.bfloat16

D_MODEL = 1024
HEAD_DIM = 64
N_HEADS = 8
BRANCH_W = N_HEADS * HEAD_DIM
LRU_WIDTH = 512
LRU_BLOCKS = 8
CONV_WIDTH = 4
LRU_C = 8.0
SWA_KV_HEADS = 2
SWA_KW = SWA_KV_HEADS * HEAD_DIM
SWA_WINDOW = 128
DIL_NEAR_WINDOW = 128
DIL_MID_WINDOW = 512
DIL_MID_STEP = 4
DIL_FAR_STEP = 16
MEM_LEN = 256
XA_HEADS = 4
XA_HEAD_DIM = D_MODEL // XA_HEADS
D_FF = 4 * D_MODEL
ROPE_THETA = 10000.0
EPS = 1e-6
LOG2E = math.log2(math.e)
ATTN_SCALE = HEAD_DIM ** -0.5
Q_SCALE_LOG2 = ATTN_SCALE * LOG2E
XA_SCALE = XA_HEAD_DIM ** -0.5
LANES = 128
NEG_INF = float("-inf")

ROW_TILE = 512
SUB_TILE = 512
MLP_TILE = 1024
ATTN_TILE = 512
VMEM_LIMIT = 48 * 1024 * 1024


def _params(*sem):
    return pltpu.CompilerParams(dimension_semantics=sem, vmem_limit_bytes=VMEM_LIMIT)


def _resident(shape):
    zeros = (0,) * len(shape)
    return pl.BlockSpec(shape, lambda *_: zeros, pipeline_mode=pl.Buffered(1))


def _rows(tile, width):
    return pl.BlockSpec((tile, width), lambda i: (i, 0))


def _rmsnorm(x, g):
    return x * lax.rsqrt(jnp.mean(x * x, axis=-1, keepdims=True) + EPS) * g


def _mm(a, b):
    return jnp.dot(a, b, preferred_element_type=F32)


def _mm_nt(a, b):
    return lax.dot_general(a, b, (((1,), (1,)), ((), ())), preferred_element_type=F32)


def _inproj_cd_kernel(x_ref, g_ref, w_ref, cos_ref, sin_ref, qc_ref, kc_ref, vc_ref, qd_ref, kd_ref, vd_ref):
    subs = [slice(r * SUB_TILE, (r + 1) * SUB_TILE) for r in range(x_ref.shape[0] // SUB_TILE)]
    xn = [_rmsnorm(x_ref[rows, :], g_ref[...]).astype(BF16) for rows in subs]
    lane = lax.broadcasted_iota(jnp.int32, (SUB_TILE, LANES), 1)
    first_half = (lane % HEAD_DIM) < (HEAD_DIM // 2)

    def rope(z, rows, scale):
        cos, sin = cos_ref[rows, :], sin_ref[rows, :]
        outs = []
        for c in range(z.shape[1] // LANES):
            zc = z[:, c * LANES:(c + 1) * LANES]
            partner = jnp.where(first_half, pltpu.roll(zc, LANES - HEAD_DIM // 2, 1), pltpu.roll(zc, HEAD_DIM // 2, 1))
            outs.append(((zc * cos + partner * sin) * scale).astype(BF16))
        return outs[0] if len(outs) == 1 else jnp.concatenate(outs, axis=1)

    w = BRANCH_W
    groups = [(qc_ref, 0, w, Q_SCALE_LOG2), (kc_ref, w, 2 * w, 1.0), (vc_ref, 2 * w, 3 * w, None),
              (qd_ref, 3 * w, 4 * w, Q_SCALE_LOG2), (kd_ref, 4 * w, 4 * w + SWA_KW, 1.0),
              (vd_ref, 4 * w + SWA_KW, 4 * w + 2 * SWA_KW, None)]
    for out_ref, lo, hi, scale in groups:
        z = [_mm(x, w_ref[:, lo:hi]) for x in xn]
        for rows, zr in zip(subs, z):
            out_ref[rows, :] = zr.astype(BF16) if scale is None else rope(zr, rows, scale)


def _inproj_cd(h, g, w, cos, sin, seq):
    t = h.shape[0]
    sds = jax.ShapeDtypeStruct
    tiles_per_seq = seq // MLP_TILE
    table = pl.BlockSpec((MLP_TILE, LANES), lambda i: (i % tiles_per_seq, 0))
    return pl.pallas_call(
        _inproj_cd_kernel,
        grid=(t // MLP_TILE,),
        in_specs=[_rows(MLP_TILE, D_MODEL), _resident(g.shape), _resident(w.shape), table, table],
        out_specs=[_rows(MLP_TILE, BRANCH_W)] * 4 + [_rows(MLP_TILE, SWA_KW)] * 2,
        out_shape=[sds((t, BRANCH_W), BF16)] * 4 + [sds((t, SWA_KW), BF16)] * 2,
        compiler_params=_params("parallel"),
        name="inproj_cd",
    )(h, g, w, cos, sin)


SUBLANES = 8


def _group_rows(x):
    return x.reshape(x.shape[0] // SUBLANES, SUBLANES, x.shape[1])


def _scan_linear(a, x, h0):
    a3, x3 = _group_rows(a), _group_rows(x)
    row = lax.broadcasted_iota(jnp.int32, a3.shape, 1)
    d = 1
    while d < SUBLANES:
        keep = row >= d
        x3 = jnp.where(keep, x3 + a3 * pltpu.roll(x3, d, 1), x3)
        a3 = jnp.where(keep, a3 * pltpu.roll(a3, d, 1), a3)
        d *= 2
    out, carry = [], h0
    for g in range(a3.shape[0]):
        hg = a3[g] * carry + x3[g]
        out.append(hg)
        carry = hg[SUBLANES - 1:SUBLANES, :]
    return jnp.concatenate(out, axis=0)


def _scan_sum(x, c0):
    x3 = _group_rows(x)
    row = lax.broadcasted_iota(jnp.int32, x3.shape, 1)
    d = 1
    while d < SUBLANES:
        x3 = jnp.where(row >= d, x3 + pltpu.roll(x3, d, 1), x3)
        d *= 2
    out, carry = [], c0
    for g in range(x3.shape[0]):
        cg = x3[g] + carry
        out.append(cg)
        carry = cg[SUBLANES - 1:SUBLANES, :]
    return jnp.concatenate(out, axis=0)


def _softplus(x):
    return jnp.maximum(x, 0.0) + jnp.log1p(jnp.exp(-jnp.abs(x)))


def _sigmoid(x):
    return 1.0 / (1.0 + jnp.exp(-x))


def _gelu_tanh(x):
    return 0.5 * x * (1.0 + jnp.tanh(math.sqrt(2.0 / math.pi) * (x + 0.044715 * (x * x * x))))


CONV_TAIL = CONV_WIDTH * 2


def _reset_recurrence(ubuf, hprev, cprev):
    ubuf[0:CONV_TAIL, :] = jnp.zeros((CONV_TAIL, LRU_WIDTH), F32)
    hprev[...] = jnp.zeros_like(hprev)
    cprev[...] = jnp.zeros_like(cprev)


def _recurrent_branch(u, gate, f, cw_ref, cb_ref, wa_ref, ba_ref, wi_ref, bi_ref, lam_ref, bf_ref,
                      ubuf, hprev, cprev):
    tc = u.shape[0]
    tail = CONV_TAIL
    ubuf[tail:tail + tc, :] = u
    cw = cw_ref[...]
    conv = cb_ref[...] + cw[CONV_WIDTH - 1:CONV_WIDTH, :] * u
    for back in range(1, CONV_WIDTH):
        k = CONV_WIDTH - 1 - back
        conv = conv + cw[k:k + 1, :] * ubuf[tail - back:tail - back + tc, :]
    ubuf[0:tail, :] = u[tc - tail:tc, :]

    conv_b = conv.astype(BF16)
    r = _sigmoid(_mm(conv_b, wa_ref[...]) + ba_ref[...])
    gate_i = _sigmoid(_mm(conv_b, wi_ref[...]) + bi_ref[...])
    log_a = (-LRU_C) * r * _softplus(-lam_ref[...])
    a = jnp.exp(log_a)
    x_in = jnp.sqrt(-jnp.tanh(log_a) * (a * a + 1.0)) * (gate_i * conv)
    h = _scan_linear(a, x_in, hprev[SUBLANES - 1:SUBLANES, :])
    hprev[...] = h[tc - SUBLANES:tc, :]
    ya = h * _gelu_tanh(gate)

    z = f + bf_ref[...]
    log_f = jnp.minimum(z, 0.0) - jnp.log1p(jnp.exp(-jnp.abs(z)))
    c = _scan_sum(log_f, cprev[SUBLANES - 1:SUBLANES, :])
    cprev[...] = c[tc - SUBLANES:tc, :]
    return ya, c


def _inproj_ab_kernel(x_ref, g_ref, w_ref, wf_ref, cw_ref, cb_ref, wa_ref, ba_ref, wi_ref, bi_ref, lam_ref, bf_ref,
                      ya_ref, q_ref, k_ref, v_ref, c_ref, ubuf, hprev, cprev, *, tiles_per_seq):
    pl.when(pl.program_id(0) % tiles_per_seq == 0)(lambda: _reset_recurrence(ubuf, hprev, cprev))
    xn = _rmsnorm(x_ref[...], g_ref[...]).astype(BF16)
    w = BRANCH_W
    u = _mm(xn, w_ref[:, 0:w])
    gate = _mm(xn, w_ref[:, w:2 * w])
    f = _mm(xn, wf_ref[...])
    q_ref[...] = (_mm(xn, w_ref[:, 2 * w:3 * w]) * Q_SCALE_LOG2).astype(BF16)
    k_ref[...] = _mm(xn, w_ref[:, 3 * w:4 * w]).astype(BF16)
    v_ref[...] = _mm(xn, w_ref[:, 4 * w:5 * w]).astype(BF16)
    ya, c = _recurrent_branch(u, gate, f, cw_ref, cb_ref, wa_ref, ba_ref, wi_ref, bi_ref, lam_ref, bf_ref,
                              ubuf, hprev, cprev)
    ya_ref[...] = ya.astype(BF16)
    c_ref[...] = c


def _inproj_ab(h, g, w_main, w_f, conv_w, conv_b, wa, ba, wi, bi, lam, bf, seq):
    t = h.shape[0]
    sds = jax.ShapeDtypeStruct
    small = [g, w_main, w_f, conv_w, conv_b, wa, ba, wi, bi, lam, bf]
    return pl.pallas_call(
        functools.partial(_inproj_ab_kernel, tiles_per_seq=seq // ROW_TILE),
        grid=(t // ROW_TILE,),
        in_specs=[_rows(ROW_TILE, D_MODEL)] + [_resident(a.shape) for a in small],
        out_specs=[_rows(ROW_TILE, BRANCH_W)] * 4 + [_rows(ROW_TILE, LANES)],
        out_shape=[sds((t, BRANCH_W), BF16)] * 4 + [sds((t, LANES), F32)],
        scratch_shapes=[pltpu.VMEM((ROW_TILE + CONV_TAIL, LRU_WIDTH), F32),
                        pltpu.VMEM((SUBLANES, LRU_WIDTH), F32), pltpu.VMEM((SUBLANES, LANES), F32)],
        compiler_params=_params("arbitrary"),
        name="inproj_ab_rglru",
    )(h, *small)


HEAD_PAIRS = N_HEADS // 2


def _pair(ref, rows, p):
    return ref[rows, p * LANES:(p + 1) * LANES]


def _low_head(shape):
    return lax.broadcasted_iota(jnp.int32, shape, 1) < HEAD_DIM


def _split_pair(x):
    low = _low_head(x.shape)
    zero = jnp.zeros_like(x)
    return jnp.where(low, x, zero), jnp.where(low, zero, x)


def _pair_values(v, ones_ref):
    keys = v.shape[0]
    v_lo, v_hi = _split_pair(v)
    return jnp.concatenate([jnp.concatenate([v_lo, ones_ref[0, :keys]], axis=1),
                            jnp.concatenate([v_hi, ones_ref[1, :keys]], axis=1)], axis=0)


def _head_ones(keys):
    low = (jnp.arange(LANES) < HEAD_DIM).astype(BF16)
    return jnp.broadcast_to(jnp.stack([low, 1 - low])[:, None, :], (2, keys, LANES))


def _attn_begin(q_ref, qm_s):
    for p in range(HEAD_PAIRS):
        qm_s[2 * p], qm_s[2 * p + 1] = _split_pair(_pair(q_ref, slice(None), p))


def _attn_block(qm_s, k_ref, v_ref, ones_ref, m_s, acc_s, start, width, bias_fn, q_rows=slice(None), first=False):
    rows = pl.ds(start, width)
    nq = len(range(*q_rows.indices(qm_s.shape[1])))
    low_q = _low_head((nq, LANES))
    for p in range(HEAD_PAIRS):
        kp = _pair(k_ref, rows, p)
        rhs = _pair_values(_pair(v_ref, rows, p), ones_ref)
        probs, drops = [], []
        for e in range(2):
            h = 2 * p + e
            s = _mm_nt(qm_s[h, q_rows], kp) + bias_fn(h)
            m_new = jnp.max(s, axis=-1, keepdims=True)
            if first:
                m_new = jnp.broadcast_to(m_new, (nq, LANES))
            else:
                m_prev = m_s[h, q_rows]
                m_new = jnp.maximum(m_prev, m_new)
                drops.append(m_prev - m_new)
            m_s[h, q_rows] = m_new
            probs.append(jnp.exp2(s - jnp.concatenate([m_new] * (width // LANES), axis=1)).astype(BF16))
        update = _mm(jnp.concatenate(probs, axis=1), rhs)
        if not first:
            alpha = jnp.exp2(jnp.where(low_q, drops[0], drops[1]))
            update = acc_s[p, q_rows] * jnp.concatenate([alpha, alpha], axis=1) + update
        acc_s[p, q_rows] = update


def _attn_diagonal(state, i, tq, bias_fn):
    half = tq // 2
    first_keys = pl.multiple_of(i * tq, tq)
    second_keys = pl.multiple_of(i * tq + half, half)
    _attn_block(*state, first_keys, half, lambda h: bias_fn(h, pl.ds(first_keys, half), 0, 0), first=True)
    _attn_block(*state, second_keys, half, lambda h: bias_fn(h, pl.ds(second_keys, half), half, half),
                q_rows=slice(half, tq))


def _attn_finish(o_ref, acc_s):
    for p in range(HEAD_PAIRS):
        acc = acc_s[p]
        o_ref[:, p * LANES:(p + 1) * LANES] = (acc[:, :LANES] / acc[:, LANES:]).astype(BF16)


def _attn_scratch(tq):
    return [pltpu.VMEM((N_HEADS, tq, LANES), BF16), pltpu.VMEM((N_HEADS, tq, LANES), F32),
            pltpu.VMEM((HEAD_PAIRS, tq, 2 * LANES), F32)]


def _fox_kernel(q_ref, k_ref, v_ref, ones_ref, c_ref, o_ref, qm_s, m_s, acc_s):
    tq = q_ref.shape[0]
    half = tq // 2
    i = pl.program_id(1)
    state = (qm_s, k_ref, v_ref, ones_ref, m_s, acc_s)
    _attn_begin(q_ref, qm_s)
    row = lax.broadcasted_iota(jnp.int32, (tq, half), 0)
    col = lax.broadcasted_iota(jnp.int32, (tq, half), 1)
    causal = jnp.where(row >= col, 0.0, NEG_INF)

    def gate_bias(h, key_rows):
        return c_ref[h:h + 1, key_rows] * (-LOG2E)

    _attn_diagonal(state, i, tq, lambda h, key_rows, q_lo, k_lo: (causal if q_lo == k_lo == 0 else causal[:half])
                   + gate_bias(h, key_rows))

    def full_block(j, _):
        start = pl.multiple_of(j * tq, tq)
        _attn_block(*state, start, tq, lambda h: gate_bias(h, pl.ds(start, tq)))
        return 0

    lax.fori_loop(0, i, full_block, 0)
    _attn_finish(o_ref, acc_s)


def _fox(q, k, v, c_rows, batch, seq):
    t = q.shape[0]
    tq = ATTN_TILE
    nq = seq // tq
    seq_block = pl.BlockSpec((None, seq, BRANCH_W), lambda b, i: (b, 0, 0))
    return pl.pallas_call(
        _fox_kernel,
        grid=(batch, nq),
        in_specs=[pl.BlockSpec((tq, BRANCH_W), lambda b, i: (b * nq + i, 0)), seq_block, seq_block,
                  _resident((2, tq, LANES)), pl.BlockSpec((None, N_HEADS, seq), lambda b, i: (b, 0, 0))],
        out_specs=pl.BlockSpec((tq, BRANCH_W), lambda b, i: (b * nq + i, 0)),
        out_shape=jax.ShapeDtypeStruct((t, BRANCH_W), BF16),
        scratch_shapes=_attn_scratch(tq),
        compiler_params=_params("arbitrary", "arbitrary"),
        name="fox_attention",
    )(q, k.reshape(batch, seq, BRANCH_W), v.reshape(batch, seq, BRANCH_W), _head_ones(tq), c_rows)


DIL_NEAR_BLOCKS = -(-DIL_MID_WINDOW // ATTN_TILE)


def _dilated_kernel(q_ref, k_ref, v_ref, ones_ref, o_ref, qm_s, m_s, acc_s, bias_s):
    tq = q_ref.shape[0]
    i = pl.program_id(1)
    far = DIL_NEAR_BLOCKS + 1

    @pl.when((pl.program_id(0) == 0) & (i == 0))
    def _():
        row = lax.broadcasted_iota(jnp.int32, (tq, tq), 0)
        col = lax.broadcasted_iota(jnp.int32, (tq, tq), 1)
        diff = row - col
        for delta in range(far):
            dist = diff + delta * tq
            ok = dist >= 0
            count = ((ok & (dist <= DIL_NEAR_WINDOW)).astype(F32)
                     + (ok & (dist <= DIL_MID_WINDOW) & (dist % DIL_MID_STEP == 0)).astype(F32)
                     + (ok & (dist % DIL_FAR_STEP == 0)).astype(F32))
            bias_s[delta] = jnp.log2(count)
        bias_s[far] = jnp.where(diff % DIL_FAR_STEP == 0, 0.0, NEG_INF)

    state = (qm_s, k_ref, v_ref, ones_ref, m_s, acc_s)
    _attn_begin(q_ref, qm_s)
    _attn_diagonal(state, i, tq, lambda h, key_rows, q_lo, k_lo: bias_s[0, q_lo:, k_lo:k_lo + tq // 2])
    for delta in range(1, far):
        @pl.when(i >= delta)
        def _(delta=delta):
            _attn_block(*state, pl.multiple_of((i - delta) * tq, tq), tq, lambda h: bias_s[delta])

    def far_block(j, _):
        _attn_block(*state, pl.multiple_of(j * tq, tq), tq, lambda h: bias_s[far])
        return 0

    lax.fori_loop(0, jnp.maximum(i - DIL_NEAR_BLOCKS, 0), far_block, 0)
    _attn_finish(o_ref, acc_s)


def _dilated(q, k, v, batch, seq):
    t = q.shape[0]
    tq = ATTN_TILE
    nq = seq // tq
    seq_block = pl.BlockSpec((None, seq, BRANCH_W), lambda b, i: (b, 0, 0))
    return pl.pallas_call(
        _dilated_kernel,
        grid=(batch, nq),
        in_specs=[pl.BlockSpec((tq, BRANCH_W), lambda b, i: (b * nq + i, 0)), seq_block, seq_block,
                  _resident((2, tq, LANES))],
        out_specs=pl.BlockSpec((tq, BRANCH_W), lambda b, i: (b * nq + i, 0)),
        out_shape=jax.ShapeDtypeStruct((t, BRANCH_W), BF16),
        scratch_shapes=_attn_scratch(tq) + [pltpu.VMEM((DIL_NEAR_BLOCKS + 2, tq, tq), F32)],
        compiler_params=_params("arbitrary", "arbitrary"),
        name="dilated_attention",
    )(q, k.reshape(batch, seq, BRANCH_W), v.reshape(batch, seq, BRANCH_W), _head_ones(tq))


SWA_GROUP = N_HEADS // SWA_KV_HEADS
SWA_HEAD_ORDER = tuple(g * SWA_GROUP + p for p in range(SWA_GROUP) for g in range(SWA_KV_HEADS))


SWA_TILE = 512


def _swa_kernel(sink_ref, q_ref, kp_ref, kc_ref, vp_ref, vc_ref, ones_ref, o_ref):
    w = SWA_WINDOW
    i = pl.program_id(1)
    row = lax.broadcasted_iota(jnp.int32, (w, 2 * w), 0)
    col = lax.broadcasted_iota(jnp.int32, (w, 2 * w), 1)
    dist = row + w - col
    in_window = (dist >= 0) & (dist < SWA_WINDOW)
    bias = jnp.where(in_window, 0.0, NEG_INF)
    bias_first = jnp.where(in_window & ((col >= w) | (i > 0)), 0.0, NEG_INF)
    k_all = jnp.concatenate([kp_ref[...], kc_ref[...]], axis=0)
    v_all = jnp.concatenate([vp_ref[...], vc_ref[...]], axis=0)
    low_q = _low_head((w, LANES))
    for j in range(q_ref.shape[0] // w):
        k = k_all[j * w:(j + 2) * w]
        rhs = _pair_values(v_all[j * w:(j + 2) * w], ones_ref)
        for p in range(SWA_GROUP):
            probs, sink_drop = [], []
            for g, qm in enumerate(_split_pair(q_ref[j * w:(j + 1) * w, p * LANES:(p + 1) * LANES])):
                sink = sink_ref[g * SWA_GROUP + p] * LOG2E
                s = _mm_nt(qm, k) + (bias_first if j == 0 else bias)
                m = jnp.maximum(jnp.max(s, axis=-1, keepdims=True), sink)
                probs.append(jnp.exp2(s - m).astype(BF16))
                sink_drop.append(sink - m)
            out = _mm(jnp.concatenate(probs, axis=1), rhs)
            den = out[:, LANES:] + jnp.exp2(jnp.where(low_q, sink_drop[0], sink_drop[1]))
            o_ref[j * w:(j + 1) * w, p * LANES:(p + 1) * LANES] = (out[:, :LANES] / den).astype(BF16)


def _swa(q, k, v, sink, batch, seq):
    t = q.shape[0]
    tq = SWA_TILE
    nq = seq // tq
    sub = tq // SWA_WINDOW
    cur = pl.BlockSpec((tq, SWA_KW), lambda b, i: (b * nq + i, 0))
    prev = pl.BlockSpec((SWA_WINDOW, SWA_KW), lambda b, i: ((b * nq + i) * sub - jnp.minimum(i, 1), 0))
    return pl.pallas_call(
        _swa_kernel,
        grid=(batch, nq),
        in_specs=[pl.BlockSpec(memory_space=pltpu.SMEM),
                  pl.BlockSpec((tq, BRANCH_W), lambda b, i: (b * nq + i, 0)), prev, cur, prev, cur,
                  _resident((2, 2 * SWA_WINDOW, LANES))],
        out_specs=pl.BlockSpec((tq, BRANCH_W), lambda b, i: (b * nq + i, 0)),
        out_shape=jax.ShapeDtypeStruct((t, BRANCH_W), BF16),
        compiler_params=_params("parallel", "arbitrary"),
        name="swa_attention",
    )(sink, q, k, k, v, v, _head_ones(2 * SWA_WINDOW))


def _memkv_kernel(x_ref, g_ref, w_ref, o_ref):
    xn = _rmsnorm(x_ref[...], g_ref[...]).astype(BF16)
    o_ref[...] = _mm(xn, w_ref[...]).astype(BF16)


def _memkv(mem, g, w):
    t = mem.shape[0]
    return pl.pallas_call(
        _memkv_kernel,
        grid=(t // ROW_TILE,),
        in_specs=[_rows(ROW_TILE, D_MODEL), _resident(g.shape), _resident(w.shape)],
        out_specs=_rows(ROW_TILE, 2 * D_MODEL),
        out_shape=jax.ShapeDtypeStruct((t, 2 * D_MODEL), BF16),
        compiler_params=_params("parallel"),
        name="memory_kv",
    )(mem, g, w)


def _xattn_kernel(h_ref, ya_ref, yb_ref, wa_ref, wb_ref, g_ref, wq_ref, kv_ref, wo_ref, o_ref):
    subs = [slice(r * SUB_TILE, (r + 1) * SUB_TILE) for r in range(h_ref.shape[0] // SUB_TILE)]
    h = [h_ref[rows, :] + _mm(ya_ref[rows, :], wa_ref[...]) + _mm(yb_ref[rows, :], wb_ref[...]) for rows in subs]
    xn = [_rmsnorm(x, g_ref[...]).astype(BF16) for x in h]
    q = [(_mm(x, wq_ref[...]) * XA_SCALE).astype(BF16) for x in xn]
    outs = [[] for _ in subs]
    for hd in range(XA_HEADS):
        lo = hd * XA_HEAD_DIM
        s = [_mm_nt(x[:, lo:lo + XA_HEAD_DIM], kv_ref[:, lo:lo + XA_HEAD_DIM]) for x in q]
        p = [jnp.exp(x - jnp.max(x, axis=-1, keepdims=True)) for x in s]
        l = [jnp.sum(x, axis=-1, keepdims=True) for x in p]
        v = kv_ref[:, D_MODEL + lo:D_MODEL + lo + XA_HEAD_DIM]
        for r in range(len(subs)):
            outs[r].append((_mm(p[r].astype(BF16), v) / l[r]).astype(BF16))
    for r, rows in enumerate(subs):
        o_ref[rows, :] = h[r] + _mm(jnp.concatenate(outs[r], axis=1), wo_ref[...])


def _xattn(h, ya, yb, wa, wb, g, wq, kv, wo, seq):
    t = h.shape[0]
    tiles_per_seq = seq // MLP_TILE
    return pl.pallas_call(
        _xattn_kernel,
        grid=(t // MLP_TILE,),
        in_specs=[_rows(MLP_TILE, D_MODEL), _rows(MLP_TILE, BRANCH_W), _rows(MLP_TILE, BRANCH_W),
                  _resident(wa.shape), _resident(wb.shape), _resident(g.shape), _resident(wq.shape),
                  pl.BlockSpec((None, MEM_LEN, 2 * D_MODEL), lambda i: (i // tiles_per_seq, 0, 0)),
                  _resident(wo.shape)],
        out_specs=_rows(MLP_TILE, D_MODEL),
        out_shape=jax.ShapeDtypeStruct((t, D_MODEL), F32),
        compiler_params=_params("parallel"),
        name="memory_xattn",
    )(h, ya, yb, wa, wb, g, wq, kv, wo)


FF_CHUNK = 512


def _mlp_kernel(h_ref, g_ref, wup_ref, wdown_ref, gout_ref, o_ref, *, final_norm):
    for r in range(h_ref.shape[0] // SUB_TILE):
        rows = slice(r * SUB_TILE, (r + 1) * SUB_TILE)
        h = h_ref[rows, :]
        xn = _rmsnorm(h, g_ref[...]).astype(BF16)
        acc = h
        for c in range(D_FF // FF_CHUNK):
            cols = slice(c * FF_CHUNK, (c + 1) * FF_CHUNK)
            up = jnp.maximum(_mm(xn, wup_ref[:, cols]), 0.0)
            acc = acc + _mm((up * up).astype(BF16), wdown_ref[cols, :])
        if final_norm:
            acc = _rmsnorm(acc, gout_ref[...])
        o_ref[rows, :] = acc


def _mlp(h, g, wup, wdown, gout, final_norm):
    t = h.shape[0]
    return pl.pallas_call(
        functools.partial(_mlp_kernel, final_norm=final_norm),
        grid=(t // MLP_TILE,),
        in_specs=[_rows(MLP_TILE, D_MODEL), _resident(g.shape), _resident(wup.shape), _resident(wdown.shape),
                  _resident(gout.shape)],
        out_specs=_rows(MLP_TILE, D_MODEL),
        out_shape=jax.ShapeDtypeStruct((t, D_MODEL), F32),
        compiler_params=_params("parallel"),
        name="relu2_mlp",
    )(h, g, wup, wdown, gout)


def _block_diag(w):
    g, c, d = w.shape
    eye = jnp.eye(g, dtype=w.dtype)
    return (w[:, :, None, :] * eye[:, None, :, None]).reshape(g * c, g * d)


def _rope_tables(seq):
    half = HEAD_DIM // 2
    inv = ROPE_THETA ** (-jnp.arange(half, dtype=F32) / half)
    ang = jnp.arange(seq, dtype=F32)[:, None] * inv[None, :]
    reps = LANES // half
    cos = jnp.tile(jnp.cos(ang), (1, reps))
    sign = jnp.tile(jnp.concatenate([-jnp.ones((half,), F32), jnp.ones((half,), F32)]), LANES // HEAD_DIM)
    sin = jnp.tile(jnp.sin(ang), (1, reps)) * sign[None, :]
    return cos, sin


def _permute_heads(w, axis):
    shape = w.shape
    split = shape[:axis] + (N_HEADS, HEAD_DIM) + shape[axis + 1:]
    return jnp.take(w.reshape(split), jnp.array(SWA_HEAD_ORDER), axis=axis).reshape(shape)


def _row(v, width=None):
    v = v.astype(F32).reshape(1, -1)
    if width is not None and v.shape[1] < width:
        v = jnp.pad(v, ((0, 0), (0, width - v.shape[1])))
    return v


def kernel(x, mem, ab_norm, ab_w_in, ab_conv_w, ab_conv_b, lru_w_a, lru_b_a, lru_w_i, lru_b_i, lru_lambda, fox_b_f, ab_w_out, cd_norm, cd_w_in, cd_sink, cd_w_out, xa_norm, xa_mem_norm, xa_w_q, xa_w_kv, xa_w_o, mlp_norm, mlp_w_up, mlp_w_down, final_norm):
    batch, seq, d = x.shape
    assert d == D_MODEL and seq % ROW_TILE == 0 and seq % ATTN_TILE == 0 and seq % SWA_TILE == 0 and mem.shape[1] == MEM_LEN
    depth = xa_norm.shape[0]
    h = x.reshape(batch * seq, d)
    mem_rows = mem.reshape(batch * MEM_LEN, d)
    cos, sin = _rope_tables(seq)
    main_w = 5 * BRANCH_W

    for layer in range(depth):
        j = layer // 2
        if layer % 2 == 0:
            w_in = ab_w_in[j]
            w_f = jnp.pad(w_in[:, main_w:], ((0, 0), (0, LANES - N_HEADS))).astype(BF16)
            ya, q, k, v, c = _inproj_ab(h, _row(ab_norm[j]), w_in[:, :main_w].astype(BF16), w_f,
                                        ab_conv_w[j].astype(F32), _row(ab_conv_b[j]),
                                        _block_diag(lru_w_a[j]).astype(BF16), _row(lru_b_a[j]),
                                        _block_diag(lru_w_i[j]).astype(BF16), _row(lru_b_i[j]),
                                        _row(lru_lambda[j]), _row(fox_b_f[j], LANES), seq)
            c_rows = c.reshape(batch, seq, LANES)[:, :, :N_HEADS].transpose(0, 2, 1)
            yb = _fox(q, k, v, c_rows, batch, seq)
            w_out = ab_w_out[j].astype(BF16)
        else:
            w_in = cd_w_in[j]
            qd_lo, qd_hi = 3 * BRANCH_W, 4 * BRANCH_W
            w_in = jnp.concatenate([w_in[:, :qd_lo], _permute_heads(w_in[:, qd_lo:qd_hi], 1), w_in[:, qd_hi:]], axis=1)
            qc, kc, vc, qd, kd, vd = _inproj_cd(h, _row(cd_norm[j]), w_in.astype(BF16), cos, sin, seq)
            ya = _dilated(qc, kc, vc, batch, seq)
            yb = _swa(qd, kd, vd, cd_sink[j].astype(F32), batch, seq)
            w_out = cd_w_out[j]
            w_out = jnp.concatenate([w_out[:BRANCH_W], _permute_heads(w_out[BRANCH_W:], 0)], axis=0).astype(BF16)
        kv = _memkv(mem_rows, _row(xa_mem_norm[layer]), xa_w_kv[layer].astype(BF16))
        h = _xattn(h, ya, yb, w_out[:BRANCH_W], w_out[BRANCH_W:], _row(xa_norm[layer]), xa_w_q[layer].astype(BF16),
                   kv.reshape(batch, MEM_LEN, 2 * D_MODEL), xa_w_o[layer].astype(BF16), seq)
        h = _mlp(h, _row(mlp_norm[layer]), mlp_w_up[layer].astype(BF16), mlp_w_down[layer].astype(BF16),
                 _row(final_norm), final_norm=(layer == depth - 1))
    return h.reshape(batch, seq, d)
```

```python
import functools
import math

import jax
import jax.numpy as jnp
from jax import lax
from jax.experimental import pallas as pl
from jax.experimental.pallas import tpu as pltpu

F32 = jnp.float32
BF16 = jnp.bfloat16

D_MODEL = 1024
HEAD_DIM = 64
N_HEADS = 8
BRANCH_W = N_HEADS * HEAD_DIM
LRU_WIDTH = 512
LRU_BLOCKS = 8
CONV_WIDTH = 4
LRU_C = 8.0
SWA_KV_HEADS = 2
SWA_KW = SWA_KV_HEADS * HEAD_DIM
SWA_WINDOW = 128
DIL_NEAR_WINDOW = 128
DIL_MID_WINDOW = 512
DIL_MID_STEP = 4
DIL_FAR_STEP = 16
MEM_LEN = 256
XA_HEADS = 4
XA_HEAD_DIM = D_MODEL // XA_HEADS
D_FF = 4 * D_MODEL
ROPE_THETA = 10000.0
EPS = 1e-6
LOG2E = math.log2(math.e)
ATTN_SCALE = HEAD_DIM ** -0.5
Q_SCALE_LOG2 = ATTN_SCALE * LOG2E
XA_SCALE = XA_HEAD_DIM ** -0.5
LANES = 128
NEG_INF = float("-inf")

ROW_TILE = 512
SUB_TILE = 512
MLP_TILE = 1024
ATTN_TILE = 512
VMEM_LIMIT = 48 * 1024 * 1024


def _params(*sem):
    return pltpu.CompilerParams(dimension_semantics=sem, vmem_limit_bytes=VMEM_LIMIT)


def _resident(shape):
    zeros = (0,) * len(shape)
    return pl.BlockSpec(shape, lambda *_: zeros, pipeline_mode=pl.Buffered(1))


def _rows(tile, width):
    return pl.BlockSpec((tile, width), lambda i: (i, 0))


def _rmsnorm(x, g):
    return x * lax.rsqrt(jnp.mean(x * x, axis=-1, keepdims=True) + EPS) * g


def _mm(a, b):
    return jnp.dot(a, b, preferred_element_type=F32)


def _mm_nt(a, b):
    return lax.dot_general(a, b, (((1,), (1,)), ((), ())), preferred_element_type=F32)


def _inproj_cd_kernel(x_ref, g_ref, w_ref, cos_ref, sin_ref, qc_ref, kc_ref, vc_ref, qd_ref, kd_ref, vd_ref):
    subs = [slice(r * SUB_TILE, (r + 1) * SUB_TILE) for r in range(x_ref.shape[0] // SUB_TILE)]
    xn = [_rmsnorm(x_ref[rows, :], g_ref[...]).astype(BF16) for rows in subs]
    lane = lax.broadcasted_iota(jnp.int32, (SUB_TILE, LANES), 1)
    first_half = (lane % HEAD_DIM) < (HEAD_DIM // 2)

    def rope(z, rows, scale):
        cos, sin = cos_ref[rows, :], sin_ref[rows, :]
        outs = []
        for c in range(z.shape[1] // LANES):
            zc = z[:, c * LANES:(c + 1) * LANES]
            partner = jnp.where(first_half, pltpu.roll(zc, LANES - HEAD_DIM // 2, 1), pltpu.roll(zc, HEAD_DIM // 2, 1))
            outs.append(((zc * cos + partner * sin) * scale).astype(BF16))
        return outs[0] if len(outs) == 1 else jnp.concatenate(outs, axis=1)

    w = BRANCH_W
    groups = [(qc_ref, 0, w, Q_SCALE_LOG2), (kc_ref, w, 2 * w, 1.0), (vc_ref, 2 * w, 3 * w, None),
              (qd_ref, 3 * w, 4 * w, Q_SCALE_LOG2), (kd_ref, 4 * w, 4 * w + SWA_KW, 1.0),
              (vd_ref, 4 * w + SWA_KW, 4 * w + 2 * SWA_KW, None)]
    for out_ref, lo, hi, scale in groups:
        z = [_mm(x, w_ref[:, lo:hi]) for x in xn]
        for rows, zr in zip(subs, z):
            out_ref[rows, :] = zr.astype(BF16) if scale is None else rope(zr, rows, scale)


def _inproj_cd(h, g, w, cos, sin, seq):
    t = h.shape[0]
    sds = jax.ShapeDtypeStruct
    tiles_per_seq = seq // MLP_TILE
    table = pl.BlockSpec((MLP_TILE, LANES), lambda i: (i % tiles_per_seq, 0))
    return pl.pallas_call(
        _inproj_cd_kernel,
        grid=(t // MLP_TILE,),
        in_specs=[_rows(MLP_TILE, D_MODEL), _resident(g.shape), _resident(w.shape), table, table],
        out_specs=[_rows(MLP_TILE, BRANCH_W)] * 4 + [_rows(MLP_TILE, SWA_KW)] * 2,
        out_shape=[sds((t, BRANCH_W), BF16)] * 4 + [sds((t, SWA_KW), BF16)] * 2,
        compiler_params=_params("parallel"),
        name="inproj_cd",
    )(h, g, w, cos, sin)


SUBLANES = 8


def _group_rows(x):
    return x.reshape(x.shape[0] // SUBLANES, SUBLANES, x.shape[1])


def _scan_linear(a, x, h0):
    a3, x3 = _group_rows(a), _group_rows(x)
    row = lax.broadcasted_iota(jnp.int32, a3.shape, 1)
    d = 1
    while d < SUBLANES:
        keep = row >= d
        x3 = jnp.where(keep, x3 + a3 * pltpu.roll(x3, d, 1), x3)
        a3 = jnp.where(keep, a3 * pltpu.roll(a3, d, 1), a3)
        d *= 2
    out, carry = [], h0
    for g in range(a3.shape[0]):
        hg = a3[g] * carry + x3[g]
        out.append(hg)
        carry = hg[SUBLANES - 1:SUBLANES, :]
    return jnp.concatenate(out, axis=0)


def _scan_sum(x, c0):
    x3 = _group_rows(x)
    row = lax.broadcasted_iota(jnp.int32, x3.shape, 1)
    d = 1
    while d < SUBLANES:
        x3 = jnp.where(row >= d, x3 + pltpu.roll(x3, d, 1), x3)
        d *= 2
    out, carry = [], c0
    for g in range(x3.shape[0]):
        cg = x3[g] + carry
        out.append(cg)
        carry = cg[SUBLANES - 1:SUBLANES, :]
    return jnp.concatenate(out, axis=0)


def _softplus(x):
    return jnp.maximum(x, 0.0) + jnp.log1p(jnp.exp(-jnp.abs(x)))


def _sigmoid(x):
    return 1.0 / (1.0 + jnp.exp(-x))


def _gelu_tanh(x):
    return 0.5 * x * (1.0 + jnp.tanh(math.sqrt(2.0 / math.pi) * (x + 0.044715 * (x * x * x))))


CONV_TAIL = CONV_WIDTH * 2


def _reset_recurrence(ubuf, hprev, cprev):
    ubuf[0:CONV_TAIL, :] = jnp.zeros((CONV_TAIL, LRU_WIDTH), F32)
    hprev[...] = jnp.zeros_like(hprev)
    cprev[...] = jnp.zeros_like(cprev)


def _recurrent_branch(u, gate, f, cw_ref, cb_ref, wa_ref, ba_ref, wi_ref, bi_ref, lam_ref, bf_ref,
                      ubuf, hprev, cprev):
    tc = u.shape[0]
    tail = CONV_TAIL
    ubuf[tail:tail + tc, :] = u
    cw = cw_ref[...]
    conv = cb_ref[...] + cw[CONV_WIDTH - 1:CONV_WIDTH, :] * u
    for back in range(1, CONV_WIDTH):
        k = CONV_WIDTH - 1 - back
        conv = conv + cw[k:k + 1, :] * ubuf[tail - back:tail - back + tc, :]
    ubuf[0:tail, :] = u[tc - tail:tc, :]

    conv_b = conv.astype(BF16)
    r = _sigmoid(_mm(conv_b, wa_ref[...]) + ba_ref[...])
    gate_i = _sigmoid(_mm(conv_b, wi_ref[...]) + bi_ref[...])
    log_a = (-LRU_C) * r * _softplus(-lam_ref[...])
    a = jnp.exp(log_a)
    x_in = jnp.sqrt(-jnp.tanh(log_a) * (a * a + 1.0)) * (gate_i * conv)
    h = _scan_linear(a, x_in, hprev[SUBLANES - 1:SUBLANES, :])
    hprev[...] = h[tc - SUBLANES:tc, :]
    ya = h * _gelu_tanh(gate)

    z = f + bf_ref[...]
    log_f = jnp.minimum(z, 0.0) - jnp.log1p(jnp.exp(-jnp.abs(z)))
    c = _scan_sum(log_f, cprev[SUBLANES - 1:SUBLANES, :])
    cprev[...] = c[tc - SUBLANES:tc, :]
    return ya, c


def _inproj_ab_kernel(x_ref, g_ref, w_ref, wf_ref, cw_ref, cb_ref, wa_ref, ba_ref, wi_ref, bi_ref, lam_ref, bf_ref,
                      ya_ref, q_ref, k_ref, v_ref, c_ref, ubuf, hprev, cprev, *, tiles_per_seq):
    pl.when(pl.program_id(0) % tiles_per_seq == 0)(lambda: _reset_recurrence(ubuf, hprev, cprev))
    xn = _rmsnorm(x_ref[...], g_ref[...]).astype(BF16)
    w = BRANCH_W
    u = _mm(xn, w_ref[:, 0:w])
    gate = _mm(xn, w_ref[:, w:2 * w])
    f = _mm(xn, wf_ref[...])
    q_ref[...] = (_mm(xn, w_ref[:, 2 * w:3 * w]) * Q_SCALE_LOG2).astype(BF16)
    k_ref[...] = _mm(xn, w_ref[:, 3 * w:4 * w]).astype(BF16)
    v_ref[...] = _mm(xn, w_ref[:, 4 * w:5 * w]).astype(BF16)
    ya, c = _recurrent_branch(u, gate, f, cw_ref, cb_ref, wa_ref, ba_ref, wi_ref, bi_ref, lam_ref, bf_ref,
                              ubuf, hprev, cprev)
    ya_ref[...] = ya.astype(BF16)
    c_ref[...] = c


def _inproj_ab(h, g, w_main, w_f, conv_w, conv_b, wa, ba, wi, bi, lam, bf, seq):
    t = h.shape[0]
    sds = jax.ShapeDtypeStruct
    small = [g, w_main, w_f, conv_w, conv_b, wa, ba, wi, bi, lam, bf]
    return pl.pallas_call(
        functools.partial(_inproj_ab_kernel, tiles_per_seq=seq // ROW_TILE),
        grid=(t // ROW_TILE,),
        in_specs=[_rows(ROW_TILE, D_MODEL)] + [_resident(a.shape) for a in small],
        out_specs=[_rows(ROW_TILE, BRANCH_W)] * 4 + [_rows(ROW_TILE, LANES)],
        out_shape=[sds((t, BRANCH_W), BF16)] * 4 + [sds((t, LANES), F32)],
        scratch_shapes=[pltpu.VMEM((ROW_TILE + CONV_TAIL, LRU_WIDTH), F32),
                        pltpu.VMEM((SUBLANES, LRU_WIDTH), F32), pltpu.VMEM((SUBLANES, LANES), F32)],
        compiler_params=_params("arbitrary"),
        name="inproj_ab_rglru",
    )(h, *small)


HEAD_PAIRS = N_HEADS // 2


def _pair(ref, rows, p):
    return ref[rows, p * LANES:(p + 1) * LANES]


def _low_head(shape):
    return lax.broadcasted_iota(jnp.int32, shape, 1) < HEAD_DIM


def _split_pair(x):
    low = _low_head(x.shape)
    zero = jnp.zeros_like(x)
    return jnp.where(low, x, zero), jnp.where(low, zero, x)


def _pair_values(v, ones_ref):
    keys = v.shape[0]
    v_lo, v_hi = _split_pair(v)
    return jnp.concatenate([jnp.concatenate([v_lo, ones_ref[0, :keys]], axis=1),
                            jnp.concatenate([v_hi, ones_ref[1, :keys]], axis=1)], axis=0)


def _head_ones(keys):
    low = (jnp.arange(LANES) < HEAD_DIM).astype(BF16)
    return jnp.broadcast_to(jnp.stack([low, 1 - low])[:, None, :], (2, keys, LANES))


def _attn_begin(q_ref, qm_s):
    for p in range(HEAD_PAIRS):
        qm_s[2 * p], qm_s[2 * p + 1] = _split_pair(_pair(q_ref, slice(None), p))


def _attn_block(qm_s, k_ref, v_ref, ones_ref, m_s, acc_s, start, width, bias_fn, q_rows=slice(None), first=False):
    rows = pl.ds(start, width)
    nq = len(range(*q_rows.indices(qm_s.shape[1])))
    low_q = _low_head((nq, LANES))
    for p in range(HEAD_PAIRS):
        kp = _pair(k_ref, rows, p)
        rhs = _pair_values(_pair(v_ref, rows, p), ones_ref)
        probs, drops = [], []
        for e in range(2):
            h = 2 * p + e
            s = _mm_nt(qm_s[h, q_rows], kp) + bias_fn(h)
            m_new = jnp.max(s, axis=-1, keepdims=True)
            if first:
                m_new = jnp.broadcast_to(m_new, (nq, LANES))
            else:
                m_prev = m_s[h, q_rows]
                m_new = jnp.maximum(m_prev, m_new)
                drops.append(m_prev - m_new)
            m_s[h, q_rows] = m_new
            probs.append(jnp.exp2(s - jnp.concatenate([m_new] * (width // LANES), axis=1)).astype(BF16))
        update = _mm(jnp.concatenate(probs, axis=1), rhs)
        if not first:
            alpha = jnp.exp2(jnp.where(low_q, drops[0], drops[1]))
            update = acc_s[p, q_rows] * jnp.concatenate([alpha, alpha], axis=1) + update
        acc_s[p, q_rows] = update


def _attn_diagonal(state, i, tq, bias_fn):
    half = tq // 2
    first_keys = pl.multiple_of(i * tq, tq)
    second_keys = pl.multiple_of(i * tq + half, half)
    _attn_block(*state, first_keys, half, lambda h: bias_fn(h, pl.ds(first_keys, half), 0, 0), first=True)
    _attn_block(*state, second_keys, half, lambda h: bias_fn(h, pl.ds(second_keys, half), half, half),
                q_rows=slice(half, tq))


def _attn_finish(o_ref, acc_s):
    for p in range(HEAD_PAIRS):
        acc = acc_s[p]
        o_ref[:, p * LANES:(p + 1) * LANES] = (acc[:, :LANES] / acc[:, LANES:]).astype(BF16)


def _attn_scratch(tq):
    return [pltpu.VMEM((N_HEADS, tq, LANES), BF16), pltpu.VMEM((N_HEADS, tq, LANES), F32),
            pltpu.VMEM((HEAD_PAIRS, tq, 2 * LANES), F32)]


def _fox_kernel(q_ref, k_ref, v_ref, ones_ref, c_ref, o_ref, qm_s, m_s, acc_s):
    tq = q_ref.shape[0]
    half = tq // 2
    i = pl.program_id(1)
    state = (qm_s, k_ref, v_ref, ones_ref, m_s, acc_s)
    _attn_begin(q_ref, qm_s)
    row = lax.broadcasted_iota(jnp.int32, (tq, half), 0)
    col = lax.broadcasted_iota(jnp.int32, (tq, half), 1)
    causal = jnp.where(row >= col, 0.0, NEG_INF)

    def gate_bias(h, key_rows):
        return c_ref[h:h + 1, key_rows] * (-LOG2E)

    _attn_diagonal(state, i, tq, lambda h, key_rows, q_lo, k_lo: (causal if q_lo == k_lo == 0 else causal[:half])
                   + gate_bias(h, key_rows))

    def full_block(j, _):
        start = pl.multiple_of(j * tq, tq)
        _attn_block(*state, start, tq, lambda h: gate_bias(h, pl.ds(start, tq)))
        return 0

    lax.fori_loop(0, i, full_block, 0)
    _attn_finish(o_ref, acc_s)


def _fox(q, k, v, c_rows, batch, seq):
    t = q.shape[0]
    tq = ATTN_TILE
    nq = seq // tq
    seq_block = pl.BlockSpec((None, seq, BRANCH_W), lambda b, i: (b, 0, 0))
    return pl.pallas_call(
        _fox_kernel,
        grid=(batch, nq),
        in_specs=[pl.BlockSpec((tq, BRANCH_W), lambda b, i: (b * nq + i, 0)), seq_block, seq_block,
                  _resident((2, tq, LANES)), pl.BlockSpec((None, N_HEADS, seq), lambda b, i: (b, 0, 0))],
        out_specs=pl.BlockSpec((tq, BRANCH_W), lambda b, i: (b * nq + i, 0)),
        out_shape=jax.ShapeDtypeStruct((t, BRANCH_W), BF16),
        scratch_shapes=_attn_scratch(tq),
        compiler_params=_params("arbitrary", "arbitrary"),
        name="fox_attention",
    )(q, k.reshape(batch, seq, BRANCH_W), v.reshape(batch, seq, BRANCH_W), _head_ones(tq), c_rows)


DIL_NEAR_BLOCKS = -(-DIL_MID_WINDOW // ATTN_TILE)


def _dilated_kernel(q_ref, k_ref, v_ref, ones_ref, o_ref, qm_s, m_s, acc_s, bias_s):
    tq = q_ref.shape[0]
    i = pl.program_id(1)
    far = DIL_NEAR_BLOCKS + 1

    @pl.when((pl.program_id(0) == 0) & (i == 0))
    def _():
        row = lax.broadcasted_iota(jnp.int32, (tq, tq), 0)
        col = lax.broadcasted_iota(jnp.int32, (tq, tq), 1)
        diff = row - col
        for delta in range(far):
            dist = diff + delta * tq
            ok = dist >= 0
            count = ((ok & (dist <= DIL_NEAR_WINDOW)).astype(F32)
                     + (ok & (dist <= DIL_MID_WINDOW) & (dist % DIL_MID_STEP == 0)).astype(F32)
                     + (ok & (dist % DIL_FAR_STEP == 0)).astype(F32))
            bias_s[delta] = jnp.log2(count)
        bias_s[far] = jnp.where(diff % DIL_FAR_STEP == 0, 0.0, NEG_INF)

    state = (qm_s, k_ref, v_ref, ones_ref, m_s, acc_s)
    _attn_begin(q_ref, qm_s)
    _attn_diagonal(state, i, tq, lambda h, key_rows, q_lo, k_lo: bias_s[0, q_lo:, k_lo:k_lo + tq // 2])
    for delta in range(1, far):
        @pl.when(i >= delta)
        def _(delta=delta):
            _attn_block(*state, pl.multiple_of((i - delta) * tq, tq), tq, lambda h: bias_s[delta])

    def far_block(j, _):
        _attn_block(*state, pl.multiple_of(j * tq, tq), tq, lambda h: bias_s[far])
        return 0

    lax.fori_loop(0, jnp.maximum(i - DIL_NEAR_BLOCKS, 0), far_block, 0)
    _attn_finish(o_ref, acc_s)


def _dilated(q, k, v, batch, seq):
    t = q.shape[0]
    tq = ATTN_TILE
    nq = seq // tq
    seq_block = pl.BlockSpec((None, seq, BRANCH_W), lambda b, i: (b, 0, 0))
    return pl.pallas_call(
        _dilated_kernel,
        grid=(batch, nq),
        in_specs=[pl.BlockSpec((tq, BRANCH_W), lambda b, i: (b * nq + i, 0)), seq_block, seq_block,
                  _resident((2, tq, LANES))],
        out_specs=pl.BlockSpec((tq, BRANCH_W), lambda b, i: (b * nq + i, 0)),
        out_shape=jax.ShapeDtypeStruct((t, BRANCH_W), BF16),
        scratch_shapes=_attn_scratch(tq) + [pltpu.VMEM((DIL_NEAR_BLOCKS + 2, tq, tq), F32)],
        compiler_params=_params("arbitrary", "arbitrary"),
        name="dilated_attention",
    )(q, k.reshape(batch, seq, BRANCH_W), v.reshape(batch, seq, BRANCH_W), _head_ones(tq))


SWA_GROUP = N_HEADS // SWA_KV_HEADS
SWA_HEAD_ORDER = tuple(g * SWA_GROUP + p for p in range(SWA_GROUP) for g in range(SWA_KV_HEADS))


SWA_TILE = 512


def _swa_kernel(sink_ref, q_ref, kp_ref, kc_ref, vp_ref, vc_ref, ones_ref, o_ref):
    w = SWA_WINDOW
    i = pl.program_id(1)
    row = lax.broadcasted_iota(jnp.int32, (w, 2 * w), 0)
    col = lax.broadcasted_iota(jnp.int32, (w, 2 * w), 1)
    dist = row + w - col
    in_window = (dist >= 0) & (dist < SWA_WINDOW)
    bias = jnp.where(in_window, 0.0, NEG_INF)
    bias_first = jnp.where(in_window & ((col >= w) | (i > 0)), 0.0, NEG_INF)
    k_all = jnp.concatenate([kp_ref[...], kc_ref[...]], axis=0)
    v_all = jnp.concatenate([vp_ref[...], vc_ref[...]], axis=0)
    low_q = _low_head((w, LANES))
    for j in range(q_ref.shape[0] // w):
        k = k_all[j * w:(j + 2) * w]
        rhs = _pair_values(v_all[j * w:(j + 2) * w], ones_ref)
        for p in range(SWA_GROUP):
            probs, sink_drop = [], []
            for g, qm in enumerate(_split_pair(q_ref[j * w:(j + 1) * w, p * LANES:(p + 1) * LANES])):
                sink = sink_ref[g * SWA_GROUP + p] * LOG2E
                s = _mm_nt(qm, k) + (bias_first if j == 0 else bias)
                m = jnp.maximum(jnp.max(s, axis=-1, keepdims=True), sink)
                probs.append(jnp.exp2(s - m).astype(BF16))
                sink_drop.append(sink - m)
            out = _mm(jnp.concatenate(probs, axis=1), rhs)
            den = out[:, LANES:] + jnp.exp2(jnp.where(low_q, sink_drop[0], sink_drop[1]))
            o_ref[j * w:(j + 1) * w, p * LANES:(p + 1) * LANES] = (out[:, :LANES] / den).astype(BF16)


def _swa(q, k, v, sink, batch, seq):
    t = q.shape[0]
    tq = SWA_TILE
    nq = seq // tq
    sub = tq // SWA_WINDOW
    cur = pl.BlockSpec((tq, SWA_KW), lambda b, i: (b * nq + i, 0))
    prev = pl.BlockSpec((SWA_WINDOW, SWA_KW), lambda b, i: ((b * nq + i) * sub - jnp.minimum(i, 1), 0))
    return pl.pallas_call(
        _swa_kernel,
        grid=(batch, nq),
        in_specs=[pl.BlockSpec(memory_space=pltpu.SMEM),
                  pl.BlockSpec((tq, BRANCH_W), lambda b, i: (b * nq + i, 0)), prev, cur, prev, cur,
                  _resident((2, 2 * SWA_WINDOW, LANES))],
        out_specs=pl.BlockSpec((tq, BRANCH_W), lambda b, i: (b * nq + i, 0)),
        out_shape=jax.ShapeDtypeStruct((t, BRANCH_W), BF16),
        compiler_params=_params("parallel", "arbitrary"),
        name="swa_attention",
    )(sink, q, k, k, v, v, _head_ones(2 * SWA_WINDOW))


def _memkv_kernel(x_ref, g_ref, w_ref, o_ref):
    xn = _rmsnorm(x_ref[...], g_ref[...]).astype(BF16)
    o_ref[...] = _mm(xn, w_ref[...]).astype(BF16)


def _memkv(mem, g, w):
    t = mem.shape[0]
    return pl.pallas_call(
        _memkv_kernel,
        grid=(t // ROW_TILE,),
        in_specs=[_rows(ROW_TILE, D_MODEL), _resident(g.shape), _resident(w.shape)],
        out_specs=_rows(ROW_TILE, 2 * D_MODEL),
        out_shape=jax.ShapeDtypeStruct((t, 2 * D_MODEL), BF16),
        compiler_params=_params("parallel"),
        name="memory_kv",
    )(mem, g, w)


FF_CHUNK = 512
TAIL_VMEM_LIMIT = 58 * 1024 * 1024


def _tail_kernel(h_ref, ya_ref, yb_ref, wa_ref, wb_ref, wq_ref, kv_ref, wo_ref, wup_ref, wdown_ref, gains_ref,
                 o_ref, *, final_norm):
    gx, gm, gout = gains_ref[0:1, :], gains_ref[1:2, :], gains_ref[2:3, :]
    subs = [slice(r * SUB_TILE, (r + 1) * SUB_TILE) for r in range(h_ref.shape[0] // SUB_TILE)]
    h = [h_ref[rows, :] + _mm(ya_ref[rows, :], wa_ref[...]) + _mm(yb_ref[rows, :], wb_ref[...]) for rows in subs]
    xn = [_rmsnorm(x, gx).astype(BF16) for x in h]
    q = [(_mm(x, wq_ref[...]) * XA_SCALE).astype(BF16) for x in xn]
    outs = [[] for _ in subs]
    for hd in range(XA_HEADS):
        lo = hd * XA_HEAD_DIM
        s = [_mm_nt(x[:, lo:lo + XA_HEAD_DIM], kv_ref[:, lo:lo + XA_HEAD_DIM]) for x in q]
        p = [jnp.exp(x - jnp.max(x, axis=-1, keepdims=True)) for x in s]
        l = [jnp.sum(x, axis=-1, keepdims=True) for x in p]
        v = kv_ref[:, D_MODEL + lo:D_MODEL + lo + XA_HEAD_DIM]
        for r in range(len(subs)):
            outs[r].append((_mm(p[r].astype(BF16), v) / l[r]).astype(BF16))
    for r, rows in enumerate(subs):
        acc = h[r] + _mm(jnp.concatenate(outs[r], axis=1), wo_ref[...])
        xm = _rmsnorm(acc, gm).astype(BF16)
        for c in range(D_FF // FF_CHUNK):
            cols = slice(c * FF_CHUNK, (c + 1) * FF_CHUNK)
            up = jnp.maximum(_mm(xm, wup_ref[:, cols]), 0.0)
            acc = acc + _mm((up * up).astype(BF16), wdown_ref[cols, :])
        if final_norm:
            acc = _rmsnorm(acc, gout)
        o_ref[rows, :] = acc


def _tail(h, ya, yb, wa, wb, wq, kv, wo, wup, wdown, gains, seq, final_norm):
    t = h.shape[0]
    tiles_per_seq = seq // MLP_TILE
    return pl.pallas_call(
        functools.partial(_tail_kernel, final_norm=final_norm),
        grid=(t // MLP_TILE,),
        in_specs=[_rows(MLP_TILE, D_MODEL), _rows(MLP_TILE, BRANCH_W), _rows(MLP_TILE, BRANCH_W),
                  _resident(wa.shape), _resident(wb.shape), _resident(wq.shape),
                  pl.BlockSpec((None, MEM_LEN, 2 * D_MODEL), lambda i: (i // tiles_per_seq, 0, 0)),
                  _resident(wo.shape), _resident(wup.shape), _resident(wdown.shape), _resident(gains.shape)],
        out_specs=_rows(MLP_TILE, D_MODEL),
        out_shape=jax.ShapeDtypeStruct((t, D_MODEL), F32),
        compiler_params=pltpu.CompilerParams(dimension_semantics=("parallel",), vmem_limit_bytes=TAIL_VMEM_LIMIT),
        name="xattn_mlp",
    )(h, ya, yb, wa, wb, wq, kv, wo, wup, wdown, gains)


def _block_diag(w):
    g, c, d = w.shape
    eye = jnp.eye(g, dtype=w.dtype)
    return (w[:, :, None, :] * eye[:, None, :, None]).reshape(g * c, g * d)


def _rope_tables(seq):
    half = HEAD_DIM // 2
    inv = ROPE_THETA ** (-jnp.arange(half, dtype=F32) / half)
    ang = jnp.arange(seq, dtype=F32)[:, None] * inv[None, :]
    reps = LANES // half
    cos = jnp.tile(jnp.cos(ang), (1, reps))
    sign = jnp.tile(jnp.concatenate([-jnp.ones((half,), F32), jnp.ones((half,), F32)]), LANES // HEAD_DIM)
    sin = jnp.tile(jnp.sin(ang), (1, reps)) * sign[None, :]
    return cos, sin


def _permute_heads(w, axis):
    shape = w.shape
    split = shape[:axis] + (N_HEADS, HEAD_DIM) + shape[axis + 1:]
    return jnp.take(w.reshape(split), jnp.array(SWA_HEAD_ORDER), axis=axis).reshape(shape)


def _row(v, width=None):
    v = v.astype(F32).reshape(1, -1)
    if width is not None and v.shape[1] < width:
        v = jnp.pad(v, ((0, 0), (0, width - v.shape[1])))
    return v


def kernel(x, mem, ab_norm, ab_w_in, ab_conv_w, ab_conv_b, lru_w_a, lru_b_a, lru_w_i, lru_b_i, lru_lambda, fox_b_f, ab_w_out, cd_norm, cd_w_in, cd_sink, cd_w_out, xa_norm, xa_mem_norm, xa_w_q, xa_w_kv, xa_w_o, mlp_norm, mlp_w_up, mlp_w_down, final_norm):
    batch, seq, d = x.shape
    assert d == D_MODEL and seq % ROW_TILE == 0 and seq % ATTN_TILE == 0 and seq % SWA_TILE == 0 and mem.shape[1] == MEM_LEN
    depth = xa_norm.shape[0]
    h = x.reshape(batch * seq, d)
    mem_rows = mem.reshape(batch * MEM_LEN, d)
    cos, sin = _rope_tables(seq)
    main_w = 5 * BRANCH_W

    for layer in range(depth):
        j = layer // 2
        if layer % 2 == 0:
            w_in = ab_w_in[j]
            w_f = jnp.pad(w_in[:, main_w:], ((0, 0), (0, LANES - N_HEADS))).astype(BF16)
            ya, q, k, v, c = _inproj_ab(h, _row(ab_norm[j]), w_in[:, :main_w].astype(BF16), w_f,
                                        ab_conv_w[j].astype(F32), _row(ab_conv_b[j]),
                                        _block_diag(lru_w_a[j]).astype(BF16), _row(lru_b_a[j]),
                                        _block_diag(lru_w_i[j]).astype(BF16), _row(lru_b_i[j]),
                                        _row(lru_lambda[j]), _row(fox_b_f[j], LANES), seq)
            c_rows = c.reshape(batch, seq, LANES)[:, :, :N_HEADS].transpose(0, 2, 1)
            yb = _fox(q, k, v, c_rows, batch, seq)
            w_out = ab_w_out[j].astype(BF16)
        else:
            w_in = cd_w_in[j]
            qd_lo, qd_hi = 3 * BRANCH_W, 4 * BRANCH_W
            w_in = jnp.concatenate([w_in[:, :qd_lo], _permute_heads(w_in[:, qd_lo:qd_hi], 1), w_in[:, qd_hi:]], axis=1)
            qc, kc, vc, qd, kd, vd = _inproj_cd(h, _row(cd_norm[j]), w_in.astype(BF16), cos, sin, seq)
            ya = _dilated(qc, kc, vc, batch, seq)
            yb = _swa(qd, kd, vd, cd_sink[j].astype(F32), batch, seq)
            w_out = cd_w_out[j]
            w_out = jnp.concatenate([w_out[:BRANCH_W], _permute_heads(w_out[BRANCH_W:], 0)], axis=0).astype(BF16)
        kv = _memkv(mem_rows, _row(xa_mem_norm[layer]), xa_w_kv[layer].astype(BF16))
        gains = jnp.concatenate([_row(xa_norm[layer]), _row(mlp_norm[layer]), _row(final_norm),
                                 jnp.zeros((SUBLANES - 3, D_MODEL), F32)], axis=0)
        h = _tail(h, ya, yb, w_out[:BRANCH_W], w_out[BRANCH_W:], xa_w_q[layer].astype(BF16),
                  kv.reshape(batch, MEM_LEN, 2 * D_MODEL), xa_w_o[layer].astype(BF16),
                  mlp_w_up[layer].astype(BF16), mlp_w_down[layer].astype(BF16), gains, seq,
                  final_norm=(layer == depth - 1))
    return h.reshape(batch, seq, d)
```

```python
import functools
import math

import jax
import jax.numpy as jnp
from jax import lax
from jax.experimental import pallas as pl
from jax.experimental.pallas import tpu as pltpu

F32 = jnp.float32
BF16 = jnp.bfloat16

D_MODEL = 1024
HEAD_DIM = 64
N_HEADS = 8
BRANCH_W = N_HEADS * HEAD_DIM
LRU_WIDTH = 512
LRU_BLOCKS = 8
CONV_WIDTH = 4
LRU_C = 8.0
SWA_KV_HEADS = 2
SWA_KW = SWA_KV_HEADS * HEAD_DIM
SWA_WINDOW = 128
DIL_NEAR_WINDOW = 128
DIL_MID_WINDOW = 512
DIL_MID_STEP = 4
DIL_FAR_STEP = 16
MEM_LEN = 256
XA_HEADS = 4
XA_HEAD_DIM = D_MODEL // XA_HEADS
D_FF = 4 * D_MODEL
ROPE_THETA = 10000.0
EPS = 1e-6
LOG2E = math.log2(math.e)
ATTN_SCALE = HEAD_DIM ** -0.5
Q_SCALE_LOG2 = ATTN_SCALE * LOG2E
XA_SCALE = XA_HEAD_DIM ** -0.5
LANES = 128
NEG_INF = float("-inf")

ROW_TILE = 512
SUB_TILE = 512
MLP_TILE = 1024
ATTN_TILE = 512
VMEM_LIMIT = 48 * 1024 * 1024


def _params(*sem):
    return pltpu.CompilerParams(dimension_semantics=sem, vmem_limit_bytes=VMEM_LIMIT)


def _resident(shape):
    zeros = (0,) * len(shape)
    return pl.BlockSpec(shape, lambda *_: zeros, pipeline_mode=pl.Buffered(1))


def _rows(tile, width):
    return pl.BlockSpec((tile, width), lambda i: (i, 0))


def _rmsnorm(x, g):
    return x * lax.rsqrt(jnp.mean(x * x, axis=-1, keepdims=True) + EPS) * g


def _mm(a, b):
    return jnp.dot(a, b, preferred_element_type=F32)


def _mm_nt(a, b):
    return lax.dot_general(a, b, (((1,), (1,)), ((), ())), preferred_element_type=F32)


DIL_CLASSES = 4
CLASS_ROWS = SUB_TILE // DIL_CLASSES


def _inproj_cd_kernel(x_ref, g_ref, w_ref, cos_ref, sin_ref, qc_ref, kc_ref, vc_ref, qd_ref, kd_ref, vd_ref,
                      qc4_ref, kc4_ref, vc4_ref, z_s):
    subs = [slice(r * SUB_TILE, (r + 1) * SUB_TILE) for r in range(x_ref.shape[0] // SUB_TILE)]
    xn = [_rmsnorm(x_ref[rows, :], g_ref[...]).astype(BF16) for rows in subs]
    lane = lax.broadcasted_iota(jnp.int32, (SUB_TILE, LANES), 1)
    first_half = (lane % HEAD_DIM) < (HEAD_DIM // 2)

    def rope(z, rows, scale):
        cos, sin = cos_ref[rows, :], sin_ref[rows, :]
        outs = []
        for c in range(z.shape[1] // LANES):
            zc = z[:, c * LANES:(c + 1) * LANES]
            partner = jnp.where(first_half, pltpu.roll(zc, LANES - HEAD_DIM // 2, 1), pltpu.roll(zc, HEAD_DIM // 2, 1))
            outs.append((zc * cos + partner * sin) * scale)
        return outs

    def lane_tiles(z):
        return [z[:, c * LANES:(c + 1) * LANES] for c in range(z.shape[1] // LANES)]

    w = BRANCH_W
    groups = [(qc_ref, qc4_ref, 0, w, Q_SCALE_LOG2), (kc_ref, kc4_ref, w, 2 * w, 1.0), (vc_ref, vc4_ref, 2 * w, 3 * w, None),
              (qd_ref, None, 3 * w, 4 * w, Q_SCALE_LOG2), (kd_ref, None, 4 * w, 4 * w + SWA_KW, 1.0),
              (vd_ref, None, 4 * w + SWA_KW, 4 * w + 2 * SWA_KW, None)]
    for out_ref, class_ref, lo, hi, scale in groups:
        z = [_mm(x, w_ref[:, lo:hi]) for x in xn]
        for r, (rows, zr) in enumerate(zip(subs, z)):
            tiles = lane_tiles(zr) if scale is None else rope(zr, rows, scale)
            out_ref[rows, :] = (tiles[0] if len(tiles) == 1 else jnp.concatenate(tiles, axis=1)).astype(BF16)
            if class_ref is not None:
                for c, t in enumerate(tiles):
                    z_s[r, c] = t
                for cls in range(DIL_CLASSES):
                    picked = [z_s[r, c, pl.ds(cls, CLASS_ROWS, stride=DIL_CLASSES), :] for c in range(len(tiles))]
                    class_ref[cls, r] = jnp.concatenate(picked, axis=1).astype(BF16)


def _inproj_cd(h, g, w, cos, sin, batch, seq):
    t = h.shape[0]
    sds = jax.ShapeDtypeStruct
    tiles_per_seq = seq // MLP_TILE
    subs = MLP_TILE // SUB_TILE
    table = pl.BlockSpec((MLP_TILE, LANES), lambda i: (i % tiles_per_seq, 0))
    wide, narrow = _rows(MLP_TILE, BRANCH_W), _rows(MLP_TILE, SWA_KW)
    by_class = pl.BlockSpec((None, DIL_CLASSES, subs, CLASS_ROWS, BRANCH_W),
                            lambda i: (i // tiles_per_seq, 0, i % tiles_per_seq, 0, 0))
    class_shape = sds((batch, DIL_CLASSES, seq // SUB_TILE, CLASS_ROWS, BRANCH_W), BF16)
    return pl.pallas_call(
        _inproj_cd_kernel,
        grid=(t // MLP_TILE,),
        in_specs=[_rows(MLP_TILE, D_MODEL), _resident(g.shape), _resident(w.shape), table, table],
        out_specs=[wide] * 4 + [narrow] * 2 + [by_class] * 3,
        out_shape=[sds((t, BRANCH_W), BF16)] * 4 + [sds((t, SWA_KW), BF16)] * 2 + [class_shape] * 3,
        scratch_shapes=[pltpu.VMEM((subs, BRANCH_W // LANES, SUB_TILE, LANES), F32)],
        compiler_params=_params("parallel"),
        name="inproj_cd",
    )(h, g, w, cos, sin)


SUBLANES = 8


def _group_rows(x):
    return x.reshape(x.shape[0] // SUBLANES, SUBLANES, x.shape[1])


def _scan_linear(a, x, h0):
    a3, x3 = _group_rows(a), _group_rows(x)
    row = lax.broadcasted_iota(jnp.int32, a3.shape, 1)
    d = 1
    while d < SUBLANES:
        keep = row >= d
        x3 = jnp.where(keep, x3 + a3 * pltpu.roll(x3, d, 1), x3)
        a3 = jnp.where(keep, a3 * pltpu.roll(a3, d, 1), a3)
        d *= 2
    out, carry = [], h0
    for g in range(a3.shape[0]):
        hg = a3[g] * carry + x3[g]
        out.append(hg)
        carry = hg[SUBLANES - 1:SUBLANES, :]
    return jnp.concatenate(out, axis=0)


def _scan_sum(x, c0):
    x3 = _group_rows(x)
    row = lax.broadcasted_iota(jnp.int32, x3.shape, 1)
    d = 1
    while d < SUBLANES:
        x3 = jnp.where(row >= d, x3 + pltpu.roll(x3, d, 1), x3)
        d *= 2
    out, carry = [], c0
    for g in range(x3.shape[0]):
        cg = x3[g] + carry
        out.append(cg)
        carry = cg[SUBLANES - 1:SUBLANES, :]
    return jnp.concatenate(out, axis=0)


def _softplus(x):
    return jnp.maximum(x, 0.0) + jnp.log1p(jnp.exp(-jnp.abs(x)))


def _sigmoid(x):
    return 1.0 / (1.0 + jnp.exp(-x))


def _gelu_tanh(x):
    return 0.5 * x * (1.0 + jnp.tanh(math.sqrt(2.0 / math.pi) * (x + 0.044715 * (x * x * x))))


CONV_TAIL = CONV_WIDTH * 2


def _reset_recurrence(ubuf, hprev, cprev):
    ubuf[0:CONV_TAIL, :] = jnp.zeros((CONV_TAIL, LRU_WIDTH), F32)
    hprev[...] = jnp.zeros_like(hprev)
    cprev[...] = jnp.zeros_like(cprev)


def _recurrent_branch(u, gate, f, cw_ref, cb_ref, wa_ref, ba_ref, wi_ref, bi_ref, lam_ref, bf_ref,
                      ubuf, hprev, cprev):
    tc = u.shape[0]
    tail = CONV_TAIL
    ubuf[tail:tail + tc, :] = u
    cw = cw_ref[...]
    conv = cb_ref[...] + cw[CONV_WIDTH - 1:CONV_WIDTH, :] * u
    for back in range(1, CONV_WIDTH):
        k = CONV_WIDTH - 1 - back
        conv = conv + cw[k:k + 1, :] * ubuf[tail - back:tail - back + tc, :]
    ubuf[0:tail, :] = u[tc - tail:tc, :]

    conv_b = conv.astype(BF16)
    r = _sigmoid(_mm(conv_b, wa_ref[...]) + ba_ref[...])
    gate_i = _sigmoid(_mm(conv_b, wi_ref[...]) + bi_ref[...])
    log_a = (-LRU_C) * r * _softplus(-lam_ref[...])
    a = jnp.exp(log_a)
    x_in = jnp.sqrt(-jnp.tanh(log_a) * (a * a + 1.0)) * (gate_i * conv)
    h = _scan_linear(a, x_in, hprev[SUBLANES - 1:SUBLANES, :])
    hprev[...] = h[tc - SUBLANES:tc, :]
    ya = h * _gelu_tanh(gate)

    z = f + bf_ref[...]
    log_f = jnp.minimum(z, 0.0) - jnp.log1p(jnp.exp(-jnp.abs(z)))
    c = _scan_sum(log_f, cprev[SUBLANES - 1:SUBLANES, :])
    cprev[...] = c[tc - SUBLANES:tc, :]
    return ya, c


def _inproj_ab_kernel(x_ref, g_ref, w_ref, wf_ref, cw_ref, cb_ref, wa_ref, ba_ref, wi_ref, bi_ref, lam_ref, bf_ref,
                      ya_ref, q_ref, k_ref, v_ref, c_ref, ubuf, hprev, cprev, *, tiles_per_seq):
    pl.when(pl.program_id(0) % tiles_per_seq == 0)(lambda: _reset_recurrence(ubuf, hprev, cprev))
    xn = _rmsnorm(x_ref[...], g_ref[...]).astype(BF16)
    w = BRANCH_W
    u = _mm(xn, w_ref[:, 0:w])
    gate = _mm(xn, w_ref[:, w:2 * w])
    f = _mm(xn, wf_ref[...])
    q_ref[...] = (_mm(xn, w_ref[:, 2 * w:3 * w]) * Q_SCALE_LOG2).astype(BF16)
    k_ref[...] = _mm(xn, w_ref[:, 3 * w:4 * w]).astype(BF16)
    v_ref[...] = _mm(xn, w_ref[:, 4 * w:5 * w]).astype(BF16)
    ya, c = _recurrent_branch(u, gate, f, cw_ref, cb_ref, wa_ref, ba_ref, wi_ref, bi_ref, lam_ref, bf_ref,
                              ubuf, hprev, cprev)
    ya_ref[...] = ya.astype(BF16)
    c_ref[...] = c


def _inproj_ab(h, g, w_main, w_f, conv_w, conv_b, wa, ba, wi, bi, lam, bf, seq):
    t = h.shape[0]
    sds = jax.ShapeDtypeStruct
    small = [g, w_main, w_f, conv_w, conv_b, wa, ba, wi, bi, lam, bf]
    return pl.pallas_call(
        functools.partial(_inproj_ab_kernel, tiles_per_seq=seq // ROW_TILE),
        grid=(t // ROW_TILE,),
        in_specs=[_rows(ROW_TILE, D_MODEL)] + [_resident(a.shape) for a in small],
        out_specs=[_rows(ROW_TILE, BRANCH_W)] * 4 + [_rows(ROW_TILE, LANES)],
        out_shape=[sds((t, BRANCH_W), BF16)] * 4 + [sds((t, LANES), F32)],
        scratch_shapes=[pltpu.VMEM((ROW_TILE + CONV_TAIL, LRU_WIDTH), F32),
                        pltpu.VMEM((SUBLANES, LRU_WIDTH), F32), pltpu.VMEM((SUBLANES, LANES), F32)],
        compiler_params=_params("arbitrary"),
        name="inproj_ab_rglru",
    )(h, *small)


HEAD_PAIRS = N_HEADS // 2


def _pair(ref, rows, p):
    return ref[rows, p * LANES:(p + 1) * LANES]


def _low_head(shape):
    return lax.broadcasted_iota(jnp.int32, shape, 1) < HEAD_DIM


def _split_pair(x):
    low = _low_head(x.shape)
    zero = jnp.zeros_like(x)
    return jnp.where(low, x, zero), jnp.where(low, zero, x)


def _pair_values(v, ones_ref):
    keys = v.shape[0]
    v_lo, v_hi = _split_pair(v)
    return jnp.concatenate([jnp.concatenate([v_lo, ones_ref[0, :keys]], axis=1),
                            jnp.concatenate([v_hi, ones_ref[1, :keys]], axis=1)], axis=0)


def _head_ones(keys):
    low = (jnp.arange(LANES) < HEAD_DIM).astype(BF16)
    return jnp.broadcast_to(jnp.stack([low, 1 - low])[:, None, :], (2, keys, LANES))


def _attn_begin(q_ref, qm_s):
    for p in range(HEAD_PAIRS):
        qm_s[2 * p], qm_s[2 * p + 1] = _split_pair(_pair(q_ref, slice(None), p))


def _attn_block(qm_s, k_ref, v_ref, ones_ref, m_s, acc_s, start, width, bias_fn, q_rows=slice(None), first=False):
    rows = pl.ds(start, width)
    nq = len(range(*q_rows.indices(qm_s.shape[1])))
    low_q = _low_head((nq, LANES))
    for p in range(HEAD_PAIRS):
        kp = _pair(k_ref, rows, p)
        rhs = _pair_values(_pair(v_ref, rows, p), ones_ref)
        probs, drops = [], []
        for e in range(2):
            h = 2 * p + e
            s = _mm_nt(qm_s[h, q_rows], kp) + bias_fn(h)
            m_new = jnp.max(s, axis=-1, keepdims=True)
            if first:
                m_new = jnp.broadcast_to(m_new, (nq, LANES))
            else:
                m_prev = m_s[h, q_rows]
                m_new = jnp.maximum(m_prev, m_new)
                drops.append(m_prev - m_new)
            m_s[h, q_rows] = m_new
            probs.append(jnp.exp2(s - jnp.concatenate([m_new] * (width // LANES), axis=1)).astype(BF16))
        update = _mm(jnp.concatenate(probs, axis=1), rhs)
        if not first:
            alpha = jnp.exp2(jnp.where(low_q, drops[0], drops[1]))
            update = acc_s[p, q_rows] * jnp.concatenate([alpha, alpha], axis=1) + update
        acc_s[p, q_rows] = update


def _attn_diagonal(state, i, tq, bias_fn):
    half = tq // 2
    first_keys = pl.multiple_of(i * tq, tq)
    second_keys = pl.multiple_of(i * tq + half, half)
    _attn_block(*state, first_keys, half, lambda h: bias_fn(h, pl.ds(first_keys, half), 0, 0), first=True)
    _attn_block(*state, second_keys, half, lambda h: bias_fn(h, pl.ds(second_keys, half), half, half),
                q_rows=slice(half, tq))


def _attn_finish(o_ref, acc_s):
    for p in range(HEAD_PAIRS):
        acc = acc_s[p]
        o_ref[:, p * LANES:(p + 1) * LANES] = (acc[:, :LANES] / acc[:, LANES:]).astype(BF16)


def _attn_scratch(tq):
    return [pltpu.VMEM((N_HEADS, tq, LANES), BF16), pltpu.VMEM((N_HEADS, tq, LANES), F32),
            pltpu.VMEM((HEAD_PAIRS, tq, 2 * LANES), F32)]


def _fox_kernel(q_ref, k_ref, v_ref, ones_ref, c_ref, o_ref, qm_s, m_s, acc_s):
    tq = q_ref.shape[0]
    half = tq // 2
    i = pl.program_id(1)
    state = (qm_s, k_ref, v_ref, ones_ref, m_s, acc_s)
    _attn_begin(q_ref, qm_s)
    row = lax.broadcasted_iota(jnp.int32, (tq, half), 0)
    col = lax.broadcasted_iota(jnp.int32, (tq, half), 1)
    causal = jnp.where(row >= col, 0.0, NEG_INF)

    def gate_bias(h, key_rows):
        return c_ref[h:h + 1, key_rows] * (-LOG2E)

    _attn_diagonal(state, i, tq, lambda h, key_rows, q_lo, k_lo: (causal if q_lo == k_lo == 0 else causal[:half])
                   + gate_bias(h, key_rows))

    def full_block(j, _):
        start = pl.multiple_of(j * tq, tq)
        _attn_block(*state, start, tq, lambda h: gate_bias(h, pl.ds(start, tq)))
        return 0

    lax.fori_loop(0, i, full_block, 0)
    _attn_finish(o_ref, acc_s)


def _fox(q, k, v, c_rows, batch, seq):
    t = q.shape[0]
    tq = ATTN_TILE
    nq = seq // tq
    seq_block = pl.BlockSpec((None, seq, BRANCH_W), lambda b, i: (b, 0, 0))
    return pl.pallas_call(
        _fox_kernel,
        grid=(batch, nq),
        in_specs=[pl.BlockSpec((tq, BRANCH_W), lambda b, i: (b * nq + i, 0)), seq_block, seq_block,
                  _resident((2, tq, LANES)), pl.BlockSpec((None, N_HEADS, seq), lambda b, i: (b, 0, 0))],
        out_specs=pl.BlockSpec((tq, BRANCH_W), lambda b, i: (b * nq + i, 0)),
        out_shape=jax.ShapeDtypeStruct((t, BRANCH_W), BF16),
        scratch_shapes=_attn_scratch(tq),
        compiler_params=_params("arbitrary", "arbitrary"),
        name="fox_attention",
    )(q, k.reshape(batch, seq, BRANCH_W), v.reshape(batch, seq, BRANCH_W), _head_ones(tq), c_rows)


CLASS_LEN_WINDOW = DIL_MID_WINDOW // DIL_CLASSES
CLASS_FAR_STEP = DIL_FAR_STEP // DIL_CLASSES


def _dilated_class_kernel(q_ref, k_ref, v_ref, ones_ref, acc_ref, m_ref, qm_s, m_s, acc_s, bias_s):
    tq = q_ref.shape[0]

    @pl.when((pl.program_id(0) == 0) & (pl.program_id(1) == 0))
    def _():
        row = lax.broadcasted_iota(jnp.int32, (tq, tq), 0)
        col = lax.broadcasted_iota(jnp.int32, (tq, tq), 1)
        e = row - col
        ok = e >= 0
        count = (ok & (e <= CLASS_LEN_WINDOW)).astype(F32) + (ok & (e % CLASS_FAR_STEP == 0)).astype(F32)
        bias_s[...] = jnp.log2(count)

    state = (qm_s, k_ref, v_ref, ones_ref, m_s, acc_s)
    _attn_begin(q_ref, qm_s)
    _attn_diagonal(state, 0, tq, lambda h, key_rows, q_lo, k_lo: bias_s[q_lo:, k_lo:k_lo + tq // 2])
    for p in range(HEAD_PAIRS):
        acc_ref[p] = acc_s[p]
    m_ref[...] = m_s[...]


def _dilated_class(q4, k4, v4, batch, seq):
    n = seq // DIL_CLASSES
    by_class = pl.BlockSpec((None, None, n, BRANCH_W), lambda b, c: (b, c, 0, 0))
    sds = jax.ShapeDtypeStruct
    return pl.pallas_call(
        _dilated_class_kernel,
        grid=(batch, DIL_CLASSES),
        in_specs=[by_class, by_class, by_class, _resident((2, n, LANES))],
        out_specs=[pl.BlockSpec((None, None, HEAD_PAIRS, n, 2 * LANES), lambda b, c: (b, c, 0, 0, 0)),
                   pl.BlockSpec((None, None, N_HEADS, n, LANES), lambda b, c: (b, c, 0, 0, 0))],
        out_shape=[sds((batch, DIL_CLASSES, HEAD_PAIRS, n, 2 * LANES), F32),
                   sds((batch, DIL_CLASSES, N_HEADS, n, LANES), F32)],
        scratch_shapes=_attn_scratch(n) + [pltpu.VMEM((n, n), F32)],
        compiler_params=_params("arbitrary", "arbitrary"),
        name="dilated_class_attention",
    )(q4, k4, v4, _head_ones(n))


def _dilated_band_kernel(q_ref, kp_ref, kc_ref, vp_ref, vc_ref, ones_ref, acc4_ref, m4_ref, o_ref, acc_t, m_t):
    w = DIL_NEAR_WINDOW
    tq = q_ref.shape[0]
    n = tq // DIL_CLASSES
    i = pl.program_id(1)
    for cls in range(DIL_CLASSES):
        token_rows = pl.ds(cls, n, stride=DIL_CLASSES)
        for h in range(N_HEADS):
            m_t[h, token_rows, :] = m4_ref[cls, h]
        for p in range(HEAD_PAIRS):
            acc = acc4_ref[cls, p]
            acc_t[p, 0, token_rows, :] = acc[:, :LANES]
            acc_t[p, 1, token_rows, :] = acc[:, LANES:]

    row = lax.broadcasted_iota(jnp.int32, (w, 2 * w), 0)
    col = lax.broadcasted_iota(jnp.int32, (w, 2 * w), 1)
    dist = row + w - col
    in_band = (dist >= 0) & (dist <= DIL_NEAR_WINDOW)
    bias = jnp.where(in_band, 0.0, NEG_INF)
    bias_first = jnp.where(in_band & ((col >= w) | (i > 0)), 0.0, NEG_INF)
    k_all = jnp.concatenate([kp_ref[...], kc_ref[...]], axis=0)
    v_all = jnp.concatenate([vp_ref[...], vc_ref[...]], axis=0)
    low_q = _low_head((w, LANES))
    for j in range(tq // w):
        rows = slice(j * w, (j + 1) * w)
        k = k_all[j * w:(j + 2) * w]
        for p in range(HEAD_PAIRS):
            rhs = _pair_values(v_all[j * w:(j + 2) * w, p * LANES:(p + 1) * LANES], ones_ref)
            kp = k[:, p * LANES:(p + 1) * LANES]
            probs, keep_class = [], []
            for e, qm in enumerate(_split_pair(q_ref[rows, p * LANES:(p + 1) * LANES])):
                h = 2 * p + e
                s = _mm_nt(qm, kp) + (bias_first if j == 0 else bias)
                m_c = m_t[h, rows, :]
                m = jnp.maximum(jnp.max(s, axis=-1, keepdims=True), m_c)
                probs.append(jnp.exp2(s - jnp.concatenate([m, m], axis=1)).astype(BF16))
                keep_class.append(m_c - m)
            a = jnp.exp2(jnp.where(low_q, keep_class[0], keep_class[1]))
            acc_class = jnp.concatenate([acc_t[p, 0, rows, :], acc_t[p, 1, rows, :]], axis=1)
            out = _mm(jnp.concatenate(probs, axis=1), rhs) + acc_class * jnp.concatenate([a, a], axis=1)
            o_ref[rows, p * LANES:(p + 1) * LANES] = (out[:, :LANES] / out[:, LANES:]).astype(BF16)


def _dilated_band(q, k, v, acc4, m4, batch, seq):
    t = q.shape[0]
    tq = ATTN_TILE
    nq = seq // tq
    n = tq // DIL_CLASSES
    sub = tq // DIL_NEAR_WINDOW
    tile = pl.BlockSpec((tq, BRANCH_W), lambda b, i: (b * nq + i, 0))
    prev = pl.BlockSpec((DIL_NEAR_WINDOW, BRANCH_W), lambda b, i: ((b * nq + i) * sub - jnp.minimum(i, 1), 0))
    return pl.pallas_call(
        _dilated_band_kernel,
        grid=(batch, nq),
        in_specs=[tile, prev, tile, prev, tile, _resident((2, 2 * DIL_NEAR_WINDOW, LANES)),
                  pl.BlockSpec((None, DIL_CLASSES, HEAD_PAIRS, n, 2 * LANES), lambda b, i: (b, 0, 0, i, 0)),
                  pl.BlockSpec((None, DIL_CLASSES, N_HEADS, n, LANES), lambda b, i: (b, 0, 0, i, 0))],
        out_specs=tile,
        out_shape=jax.ShapeDtypeStruct((t, BRANCH_W), BF16),
        scratch_shapes=[pltpu.VMEM((HEAD_PAIRS, 2, tq, LANES), F32), pltpu.VMEM((N_HEADS, tq, LANES), F32)],
        compiler_params=_params("parallel", "arbitrary"),
        name="dilated_band_attention",
    )(q, k, k, v, v, _head_ones(2 * DIL_NEAR_WINDOW), acc4, m4)


SWA_GROUP = N_HEADS // SWA_KV_HEADS
SWA_HEAD_ORDER = tuple(g * SWA_GROUP + p for p in range(SWA_GROUP) for g in range(SWA_KV_HEADS))


SWA_TILE = 512


def _swa_kernel(sink_ref, q_ref, kp_ref, kc_ref, vp_ref, vc_ref, ones_ref, o_ref):
    w = SWA_WINDOW
    i = pl.program_id(1)
    row = lax.broadcasted_iota(jnp.int32, (w, 2 * w), 0)
    col = lax.broadcasted_iota(jnp.int32, (w, 2 * w), 1)
    dist = row + w - col
    in_window = (dist >= 0) & (dist < SWA_WINDOW)
    bias = jnp.where(in_window, 0.0, NEG_INF)
    bias_first = jnp.where(in_window & ((col >= w) | (i > 0)), 0.0, NEG_INF)
    k_all = jnp.concatenate([kp_ref[...], kc_ref[...]], axis=0)
    v_all = jnp.concatenate([vp_ref[...], vc_ref[...]], axis=0)
    low_q = _low_head((w, LANES))
    for j in range(q_ref.shape[0] // w):
        k = k_all[j * w:(j + 2) * w]
        rhs = _pair_values(v_all[j * w:(j + 2) * w], ones_ref)
        for p in range(SWA_GROUP):
            probs, sink_drop = [], []
            for g, qm in enumerate(_split_pair(q_ref[j * w:(j + 1) * w, p * LANES:(p + 1) * LANES])):
                sink = sink_ref[g * SWA_GROUP + p] * LOG2E
                s = _mm_nt(qm, k) + (bias_first if j == 0 else bias)
                m = jnp.maximum(jnp.max(s, axis=-1, keepdims=True), sink)
                probs.append(jnp.exp2(s - m).astype(BF16))
                sink_drop.append(sink - m)
            out = _mm(jnp.concatenate(probs, axis=1), rhs)
            den = out[:, LANES:] + jnp.exp2(jnp.where(low_q, sink_drop[0], sink_drop[1]))
            o_ref[j * w:(j + 1) * w, p * LANES:(p + 1) * LANES] = (out[:, :LANES] / den).astype(BF16)


def _swa(q, k, v, sink, batch, seq):
    t = q.shape[0]
    tq = SWA_TILE
    nq = seq // tq
    sub = tq // SWA_WINDOW
    cur = pl.BlockSpec((tq, SWA_KW), lambda b, i: (b * nq + i, 0))
    prev = pl.BlockSpec((SWA_WINDOW, SWA_KW), lambda b, i: ((b * nq + i) * sub - jnp.minimum(i, 1), 0))
    return pl.pallas_call(
        _swa_kernel,
        grid=(batch, nq),
        in_specs=[pl.BlockSpec(memory_space=pltpu.SMEM),
                  pl.BlockSpec((tq, BRANCH_W), lambda b, i: (b * nq + i, 0)), prev, cur, prev, cur,
                  _resident((2, 2 * SWA_WINDOW, LANES))],
        out_specs=pl.BlockSpec((tq, BRANCH_W), lambda b, i: (b * nq + i, 0)),
        out_shape=jax.ShapeDtypeStruct((t, BRANCH_W), BF16),
        compiler_params=_params("parallel", "arbitrary"),
        name="swa_attention",
    )(sink, q, k, k, v, v, _head_ones(2 * SWA_WINDOW))


def _memkv_kernel(x_ref, g_ref, w_ref, o_ref):
    xn = _rmsnorm(x_ref[...], g_ref[...]).astype(BF16)
    o_ref[...] = _mm(xn, w_ref[...]).astype(BF16)


def _memkv(mem, g, w):
    t = mem.shape[0]
    return pl.pallas_call(
        _memkv_kernel,
        grid=(t // ROW_TILE,),
        in_specs=[_rows(ROW_TILE, D_MODEL), _resident(g.shape), _resident(w.shape)],
        out_specs=_rows(ROW_TILE, 2 * D_MODEL),
        out_shape=jax.ShapeDtypeStruct((t, 2 * D_MODEL), BF16),
        compiler_params=_params("parallel"),
        name="memory_kv",
    )(mem, g, w)


FF_CHUNK = 512
TAIL_VMEM_LIMIT = 58 * 1024 * 1024


def _tail_kernel(h_ref, ya_ref, yb_ref, wa_ref, wb_ref, wq_ref, kv_ref, wo_ref, wup_ref, wdown_ref, gains_ref,
                 o_ref, *, final_norm):
    gx, gm, gout = gains_ref[0:1, :], gains_ref[1:2, :], gains_ref[2:3, :]
    subs = [slice(r * SUB_TILE, (r + 1) * SUB_TILE) for r in range(h_ref.shape[0] // SUB_TILE)]
    h = [h_ref[rows, :] + _mm(ya_ref[rows, :], wa_ref[...]) + _mm(yb_ref[rows, :], wb_ref[...]) for rows in subs]
    xn = [_rmsnorm(x, gx).astype(BF16) for x in h]
    q = [(_mm(x, wq_ref[...]) * XA_SCALE).astype(BF16) for x in xn]
    outs = [[] for _ in subs]
    for hd in range(XA_HEADS):
        lo = hd * XA_HEAD_DIM
        s = [_mm_nt(x[:, lo:lo + XA_HEAD_DIM], kv_ref[:, lo:lo + XA_HEAD_DIM]) for x in q]
        p = [jnp.exp(x - jnp.max(x, axis=-1, keepdims=True)) for x in s]
        l = [jnp.sum(x, axis=-1, keepdims=True) for x in p]
        v = kv_ref[:, D_MODEL + lo:D_MODEL + lo + XA_HEAD_DIM]
        for r in range(len(subs)):
            outs[r].append((_mm(p[r].astype(BF16), v) / l[r]).astype(BF16))
    for r, rows in enumerate(subs):
        acc = h[r] + _mm(jnp.concatenate(outs[r], axis=1), wo_ref[...])
        xm = _rmsnorm(acc, gm).astype(BF16)
        for c in range(D_FF // FF_CHUNK):
            cols = slice(c * FF_CHUNK, (c + 1) * FF_CHUNK)
            up = jnp.maximum(_mm(xm, wup_ref[:, cols]), 0.0)
            acc = acc + _mm((up * up).astype(BF16), wdown_ref[cols, :])
        if final_norm:
            acc = _rmsnorm(acc, gout)
        o_ref[rows, :] = acc


def _tail(h, ya, yb, wa, wb, wq, kv, wo, wup, wdown, gains, seq, final_norm):
    t = h.shape[0]
    tiles_per_seq = seq // MLP_TILE
    return pl.pallas_call(
        functools.partial(_tail_kernel, final_norm=final_norm),
        grid=(t // MLP_TILE,),
        in_specs=[_rows(MLP_TILE, D_MODEL), _rows(MLP_TILE, BRANCH_W), _rows(MLP_TILE, BRANCH_W),
                  _resident(wa.shape), _resident(wb.shape), _resident(wq.shape),
                  pl.BlockSpec((None, MEM_LEN, 2 * D_MODEL), lambda i: (i // tiles_per_seq, 0, 0)),
                  _resident(wo.shape), _resident(wup.shape), _resident(wdown.shape), _resident(gains.shape)],
        out_specs=_rows(MLP_TILE, D_MODEL),
        out_shape=jax.ShapeDtypeStruct((t, D_MODEL), F32),
        compiler_params=pltpu.CompilerParams(dimension_semantics=("parallel",), vmem_limit_bytes=TAIL_VMEM_LIMIT),
        name="xattn_mlp",
    )(h, ya, yb, wa, wb, wq, kv, wo, wup, wdown, gains)


def _block_diag(w):
    g, c, d = w.shape
    eye = jnp.eye(g, dtype=w.dtype)
    return (w[:, :, None, :] * eye[:, None, :, None]).reshape(g * c, g * d)


def _rope_tables(seq):
    half = HEAD_DIM // 2
    inv = ROPE_THETA ** (-jnp.arange(half, dtype=F32) / half)
    ang = jnp.arange(seq, dtype=F32)[:, None] * inv[None, :]
    reps = LANES // half
    cos = jnp.tile(jnp.cos(ang), (1, reps))
    sign = jnp.tile(jnp.concatenate([-jnp.ones((half,), F32), jnp.ones((half,), F32)]), LANES // HEAD_DIM)
    sin = jnp.tile(jnp.sin(ang), (1, reps)) * sign[None, :]
    return cos, sin


def _permute_heads(w, axis):
    shape = w.shape
    split = shape[:axis] + (N_HEADS, HEAD_DIM) + shape[axis + 1:]
    return jnp.take(w.reshape(split), jnp.array(SWA_HEAD_ORDER), axis=axis).reshape(shape)


def _row(v, width=None):
    v = v.astype(F32).reshape(1, -1)
    if width is not None and v.shape[1] < width:
        v = jnp.pad(v, ((0, 0), (0, width - v.shape[1])))
    return v


def kernel(x, mem, ab_norm, ab_w_in, ab_conv_w, ab_conv_b, lru_w_a, lru_b_a, lru_w_i, lru_b_i, lru_lambda, fox_b_f, ab_w_out, cd_norm, cd_w_in, cd_sink, cd_w_out, xa_norm, xa_mem_norm, xa_w_q, xa_w_kv, xa_w_o, mlp_norm, mlp_w_up, mlp_w_down, final_norm):
    batch, seq, d = x.shape
    assert d == D_MODEL and seq % ROW_TILE == 0 and seq % ATTN_TILE == 0 and seq // DIL_CLASSES == ATTN_TILE and seq % SWA_TILE == 0 and mem.shape[1] == MEM_LEN
    depth = xa_norm.shape[0]
    h = x.reshape(batch * seq, d)
    mem_rows = mem.reshape(batch * MEM_LEN, d)
    cos, sin = _rope_tables(seq)
    main_w = 5 * BRANCH_W

    for layer in range(depth):
        j = layer // 2
        if layer % 2 == 0:
            w_in = ab_w_in[j]
            w_f = jnp.pad(w_in[:, main_w:], ((0, 0), (0, LANES - N_HEADS))).astype(BF16)
            ya, q, k, v, c = _inproj_ab(h, _row(ab_norm[j]), w_in[:, :main_w].astype(BF16), w_f,
                                        ab_conv_w[j].astype(F32), _row(ab_conv_b[j]),
                                        _block_diag(lru_w_a[j]).astype(BF16), _row(lru_b_a[j]),
                                        _block_diag(lru_w_i[j]).astype(BF16), _row(lru_b_i[j]),
                                        _row(lru_lambda[j]), _row(fox_b_f[j], LANES), seq)
            c_rows = c.reshape(batch, seq, LANES)[:, :, :N_HEADS].transpose(0, 2, 1)
            yb = _fox(q, k, v, c_rows, batch, seq)
            w_out = ab_w_out[j].astype(BF16)
        else:
            w_in = cd_w_in[j]
            qd_lo, qd_hi = 3 * BRANCH_W, 4 * BRANCH_W
            w_in = jnp.concatenate([w_in[:, :qd_lo], _permute_heads(w_in[:, qd_lo:qd_hi], 1), w_in[:, qd_hi:]], axis=1)
            qc, kc, vc, qd, kd, vd, qc4, kc4, vc4 = _inproj_cd(h, _row(cd_norm[j]), w_in.astype(BF16), cos, sin, batch, seq)
            by_class = (batch, DIL_CLASSES, seq // DIL_CLASSES, BRANCH_W)
            acc4, m4 = _dilated_class(qc4.reshape(by_class), kc4.reshape(by_class), vc4.reshape(by_class), batch, seq)
            ya = _dilated_band(qc, kc, vc, acc4, m4, batch, seq)
            yb = _swa(qd, kd, vd, cd_sink[j].astype(F32), batch, seq)
            w_out = cd_w_out[j]
            w_out = jnp.concatenate([w_out[:BRANCH_W], _permute_heads(w_out[BRANCH_W:], 0)], axis=0).astype(BF16)
        kv = _memkv(mem_rows, _row(xa_mem_norm[layer]), xa_w_kv[layer].astype(BF16))
        gains = jnp.concatenate([_row(xa_norm[layer]), _row(mlp_norm[layer]), _row(final_norm),
                                 jnp.zeros((SUBLANES - 3, D_MODEL), F32)], axis=0)
        h = _tail(h, ya, yb, w_out[:BRANCH_W], w_out[BRANCH_W:], xa_w_q[layer].astype(BF16),
                  kv.reshape(batch, MEM_LEN, 2 * D_MODEL), xa_w_o[layer].astype(BF16),
                  mlp_w_up[layer].astype(BF16), mlp_w_down[layer].astype(BF16), gains, seq,
                  final_norm=(layer == depth - 1))
    return h.reshape(batch, seq, d)
```

```python
import functools
import math

import jax
import jax.numpy as jnp
from jax import lax
from jax.experimental import pallas as pl
from jax.experimental.pallas import tpu as pltpu

F32 = jnp.float32
BF16 = jnp.bfloat16

D_MODEL = 1024
HEAD_DIM = 64
N_HEADS = 8
BRANCH_W = N_HEADS * HEAD_DIM
LRU_WIDTH = 512
LRU_BLOCKS = 8
CONV_WIDTH = 4
LRU_C = 8.0
SWA_KV_HEADS = 2
SWA_KW = SWA_KV_HEADS * HEAD_DIM
SWA_WINDOW = 128
DIL_NEAR_WINDOW = 128
DIL_MID_WINDOW = 512
DIL_MID_STEP = 4
DIL_FAR_STEP = 16
MEM_LEN = 256
XA_HEADS = 4
XA_HEAD_DIM = D_MODEL // XA_HEADS
D_FF = 4 * D_MODEL
ROPE_THETA = 10000.0
EPS = 1e-6
LOG2E = math.log2(math.e)
ATTN_SCALE = HEAD_DIM ** -0.5
Q_SCALE_LOG2 = ATTN_SCALE * LOG2E
XA_SCALE = XA_HEAD_DIM ** -0.5
LANES = 128
NEG_INF = float("-inf")

ROW_TILE = 512
SUB_TILE = 512
MLP_TILE = 1024
ATTN_TILE = 512
VMEM_LIMIT = 48 * 1024 * 1024


def _params(*sem):
    return pltpu.CompilerParams(dimension_semantics=sem, vmem_limit_bytes=VMEM_LIMIT)


def _resident(shape):
    zeros = (0,) * len(shape)
    return pl.BlockSpec(shape, lambda *_: zeros, pipeline_mode=pl.Buffered(1))


def _rows(tile, width):
    return pl.BlockSpec((tile, width), lambda i: (i, 0))


def _rmsnorm(x, g):
    return x * lax.rsqrt(jnp.mean(x * x, axis=-1, keepdims=True) + EPS) * g


def _mm(a, b):
    return jnp.dot(a, b, preferred_element_type=F32)


def _mm_nt(a, b):
    return lax.dot_general(a, b, (((1,), (1,)), ((), ())), preferred_element_type=F32)


DIL_CLASSES = 4
CLASS_ROWS = SUB_TILE // DIL_CLASSES


def _inproj_cd_kernel(x_ref, g_ref, w_ref, cos_ref, sin_ref, qc_ref, kc_ref, vc_ref, qd_ref, kd_ref, vd_ref,
                      qc4_ref, kc4_ref, vc4_ref, z_s):
    subs = [slice(r * SUB_TILE, (r + 1) * SUB_TILE) for r in range(x_ref.shape[0] // SUB_TILE)]
    xn = [_rmsnorm(x_ref[rows, :], g_ref[...]).astype(BF16) for rows in subs]
    lane = lax.broadcasted_iota(jnp.int32, (SUB_TILE, LANES), 1)
    first_half = (lane % HEAD_DIM) < (HEAD_DIM // 2)

    def rope(z, rows, scale):
        cos, sin = cos_ref[rows, :], sin_ref[rows, :]
        outs = []
        for c in range(z.shape[1] // LANES):
            zc = z[:, c * LANES:(c + 1) * LANES]
            partner = jnp.where(first_half, pltpu.roll(zc, LANES - HEAD_DIM // 2, 1), pltpu.roll(zc, HEAD_DIM // 2, 1))
            outs.append((zc * cos + partner * sin) * scale)
        return outs

    def lane_tiles(z):
        return [z[:, c * LANES:(c + 1) * LANES] for c in range(z.shape[1] // LANES)]

    w = BRANCH_W
    groups = [(qc_ref, qc4_ref, 0, w, Q_SCALE_LOG2), (kc_ref, kc4_ref, w, 2 * w, 1.0), (vc_ref, vc4_ref, 2 * w, 3 * w, None),
              (qd_ref, None, 3 * w, 4 * w, Q_SCALE_LOG2), (kd_ref, None, 4 * w, 4 * w + SWA_KW, 1.0),
              (vd_ref, None, 4 * w + SWA_KW, 4 * w + 2 * SWA_KW, None)]
    for out_ref, class_ref, lo, hi, scale in groups:
        z = [_mm(x, w_ref[:, lo:hi]) for x in xn]
        for r, (rows, zr) in enumerate(zip(subs, z)):
            tiles = lane_tiles(zr) if scale is None else rope(zr, rows, scale)
            out_ref[rows, :] = (tiles[0] if len(tiles) == 1 else jnp.concatenate(tiles, axis=1)).astype(BF16)
            if class_ref is not None:
                for c, t in enumerate(tiles):
                    z_s[r, c] = t
                for cls in range(DIL_CLASSES):
                    picked = [z_s[r, c, pl.ds(cls, CLASS_ROWS, stride=DIL_CLASSES), :] for c in range(len(tiles))]
                    class_ref[cls, r] = jnp.concatenate(picked, axis=1).astype(BF16)


def _inproj_cd(h, g, w, cos, sin, batch, seq):
    t = h.shape[0]
    sds = jax.ShapeDtypeStruct
    tiles_per_seq = seq // MLP_TILE
    subs = MLP_TILE // SUB_TILE
    table = pl.BlockSpec((MLP_TILE, LANES), lambda i: (i % tiles_per_seq, 0))
    wide, narrow = _rows(MLP_TILE, BRANCH_W), _rows(MLP_TILE, SWA_KW)
    by_class = pl.BlockSpec((None, DIL_CLASSES, subs, CLASS_ROWS, BRANCH_W),
                            lambda i: (i // tiles_per_seq, 0, i % tiles_per_seq, 0, 0))
    class_shape = sds((batch, DIL_CLASSES, seq // SUB_TILE, CLASS_ROWS, BRANCH_W), BF16)
    return pl.pallas_call(
        _inproj_cd_kernel,
        grid=(t // MLP_TILE,),
        in_specs=[_rows(MLP_TILE, D_MODEL), _resident(g.shape), _resident(w.shape), table, table],
        out_specs=[wide] * 4 + [narrow] * 2 + [by_class] * 3,
        out_shape=[sds((t, BRANCH_W), BF16)] * 4 + [sds((t, SWA_KW), BF16)] * 2 + [class_shape] * 3,
        scratch_shapes=[pltpu.VMEM((subs, BRANCH_W // LANES, SUB_TILE, LANES), F32)],
        compiler_params=_params("parallel"),
        name="inproj_cd",
    )(h, g, w, cos, sin)


SUBLANES = 8


def _group_rows(x):
    return x.reshape(x.shape[0] // SUBLANES, SUBLANES, x.shape[1])


def _scan_linear(a, x, h0):
    a3, x3 = _group_rows(a), _group_rows(x)
    row = lax.broadcasted_iota(jnp.int32, a3.shape, 1)
    d = 1
    while d < SUBLANES:
        keep = row >= d
        x3 = jnp.where(keep, x3 + a3 * pltpu.roll(x3, d, 1), x3)
        a3 = jnp.where(keep, a3 * pltpu.roll(a3, d, 1), a3)
        d *= 2
    out, carry = [], h0
    for g in range(a3.shape[0]):
        hg = a3[g] * carry + x3[g]
        out.append(hg)
        carry = hg[SUBLANES - 1:SUBLANES, :]
    return jnp.concatenate(out, axis=0)


def _scan_sum(x, c0):
    x3 = _group_rows(x)
    row = lax.broadcasted_iota(jnp.int32, x3.shape, 1)
    d = 1
    while d < SUBLANES:
        x3 = jnp.where(row >= d, x3 + pltpu.roll(x3, d, 1), x3)
        d *= 2
    out, carry = [], c0
    for g in range(x3.shape[0]):
        cg = x3[g] + carry
        out.append(cg)
        carry = cg[SUBLANES - 1:SUBLANES, :]
    return jnp.concatenate(out, axis=0)


def _softplus(x):
    return jnp.maximum(x, 0.0) + jnp.log1p(jnp.exp(-jnp.abs(x)))


def _sigmoid(x):
    return 1.0 / (1.0 + jnp.exp(-x))


def _gelu_tanh(x):
    return 0.5 * x * (1.0 + jnp.tanh(math.sqrt(2.0 / math.pi) * (x + 0.044715 * (x * x * x))))


CONV_TAIL = CONV_WIDTH * 2


def _reset_recurrence(ubuf, hprev, cprev):
    ubuf[0:CONV_TAIL, :] = jnp.zeros((CONV_TAIL, LRU_WIDTH), F32)
    hprev[...] = jnp.zeros_like(hprev)
    cprev[...] = jnp.zeros_like(cprev)


def _recurrent_branch(u, gate, f, cw_ref, cb_ref, wa_ref, ba_ref, wi_ref, bi_ref, lam_ref, bf_ref,
                      ubuf, hprev, cprev):
    tc = u.shape[0]
    tail = CONV_TAIL
    ubuf[tail:tail + tc, :] = u
    cw = cw_ref[...]
    conv = cb_ref[...] + cw[CONV_WIDTH - 1:CONV_WIDTH, :] * u
    for back in range(1, CONV_WIDTH):
        k = CONV_WIDTH - 1 - back
        conv = conv + cw[k:k + 1, :] * ubuf[tail - back:tail - back + tc, :]
    ubuf[0:tail, :] = u[tc - tail:tc, :]

    conv_b = conv.astype(BF16)
    r = _sigmoid(_mm(conv_b, wa_ref[...]) + ba_ref[...])
    gate_i = _sigmoid(_mm(conv_b, wi_ref[...]) + bi_ref[...])
    log_a = (-LRU_C) * r * _softplus(-lam_ref[...])
    a = jnp.exp(log_a)
    x_in = jnp.sqrt(-jnp.tanh(log_a) * (a * a + 1.0)) * (gate_i * conv)
    h = _scan_linear(a, x_in, hprev[SUBLANES - 1:SUBLANES, :])
    hprev[...] = h[tc - SUBLANES:tc, :]
    ya = h * _gelu_tanh(gate)

    z = f + bf_ref[...]
    log_f = jnp.minimum(z, 0.0) - jnp.log1p(jnp.exp(-jnp.abs(z)))
    c = _scan_sum(log_f, cprev[SUBLANES - 1:SUBLANES, :])
    cprev[...] = c[tc - SUBLANES:tc, :]
    return ya, c


def _inproj_ab_kernel(x_ref, g_ref, w_ref, wf_ref, cw_ref, cb_ref, wa_ref, ba_ref, wi_ref, bi_ref, lam_ref, bf_ref,
                      ya_ref, q_ref, k_ref, v_ref, c_ref, ubuf, hprev, cprev, *, tiles_per_seq):
    pl.when(pl.program_id(0) % tiles_per_seq == 0)(lambda: _reset_recurrence(ubuf, hprev, cprev))
    xn = _rmsnorm(x_ref[...], g_ref[...]).astype(BF16)
    w = BRANCH_W
    u = _mm(xn, w_ref[:, 0:w])
    gate = _mm(xn, w_ref[:, w:2 * w])
    f = _mm(xn, wf_ref[...])
    q_ref[...] = (_mm(xn, w_ref[:, 2 * w:3 * w]) * Q_SCALE_LOG2).astype(BF16)
    k_ref[...] = _mm(xn, w_ref[:, 3 * w:4 * w]).astype(BF16)
    v_ref[...] = _mm(xn, w_ref[:, 4 * w:5 * w]).astype(BF16)
    ya, c = _recurrent_branch(u, gate, f, cw_ref, cb_ref, wa_ref, ba_ref, wi_ref, bi_ref, lam_ref, bf_ref,
                              ubuf, hprev, cprev)
    ya_ref[...] = ya.astype(BF16)
    c_ref[...] = c


def _inproj_ab(h, g, w_main, w_f, conv_w, conv_b, wa, ba, wi, bi, lam, bf, seq):
    t = h.shape[0]
    sds = jax.ShapeDtypeStruct
    small = [g, w_main, w_f, conv_w, conv_b, wa, ba, wi, bi, lam, bf]
    return pl.pallas_call(
        functools.partial(_inproj_ab_kernel, tiles_per_seq=seq // ROW_TILE),
        grid=(t // ROW_TILE,),
        in_specs=[_rows(ROW_TILE, D_MODEL)] + [_resident(a.shape) for a in small],
        out_specs=[_rows(ROW_TILE, BRANCH_W)] * 4 + [_rows(ROW_TILE, LANES)],
        out_shape=[sds((t, BRANCH_W), BF16)] * 4 + [sds((t, LANES), F32)],
        scratch_shapes=[pltpu.VMEM((ROW_TILE + CONV_TAIL, LRU_WIDTH), F32),
                        pltpu.VMEM((SUBLANES, LRU_WIDTH), F32), pltpu.VMEM((SUBLANES, LANES), F32)],
        compiler_params=_params("arbitrary"),
        name="inproj_ab_rglru",
    )(h, *small)


HEAD_PAIRS = N_HEADS // 2


def _pair(ref, rows, p):
    return ref[rows, p * LANES:(p + 1) * LANES]


def _low_head(shape):
    return lax.broadcasted_iota(jnp.int32, shape, 1) < HEAD_DIM


def _split_pair(x):
    low = _low_head(x.shape)
    zero = jnp.zeros_like(x)
    return jnp.where(low, x, zero), jnp.where(low, zero, x)


def _pair_values(v, ones_ref):
    keys = v.shape[0]
    v_lo, v_hi = _split_pair(v)
    return jnp.concatenate([jnp.concatenate([v_lo, ones_ref[0, :keys]], axis=1),
                            jnp.concatenate([v_hi, ones_ref[1, :keys]], axis=1)], axis=0)


def _head_ones(keys):
    low = (jnp.arange(LANES) < HEAD_DIM).astype(BF16)
    return jnp.broadcast_to(jnp.stack([low, 1 - low])[:, None, :], (2, keys, LANES))


def _attn_begin(q_ref, qm_s):
    for p in range(HEAD_PAIRS):
        qm_s[2 * p], qm_s[2 * p + 1] = _split_pair(_pair(q_ref, slice(None), p))


def _attn_block(qm_s, k_ref, v_ref, ones_ref, m_s, acc_s, start, width, bias_fn, q_rows=slice(None), first=False):
    rows = pl.ds(start, width)
    nq = len(range(*q_rows.indices(qm_s.shape[1])))
    low_q = _low_head((nq, LANES))
    for p in range(HEAD_PAIRS):
        kp = _pair(k_ref, rows, p)
        rhs = _pair_values(_pair(v_ref, rows, p), ones_ref)
        probs, drops = [], []
        for e in range(2):
            h = 2 * p + e
            s = _mm_nt(qm_s[h, q_rows], kp) + bias_fn(h)
            m_new = jnp.max(s, axis=-1, keepdims=True)
            if first:
                m_new = jnp.broadcast_to(m_new, (nq, LANES))
            else:
                m_prev = m_s[h, q_rows]
                m_new = jnp.maximum(m_prev, m_new)
                drops.append(m_prev - m_new)
            m_s[h, q_rows] = m_new
            probs.append(jnp.exp2(s - jnp.concatenate([m_new] * (width // LANES), axis=1)).astype(BF16))
        update = _mm(jnp.concatenate(probs, axis=1), rhs)
        if not first:
            alpha = jnp.exp2(jnp.where(low_q, drops[0], drops[1]))
            update = acc_s[p, q_rows] * jnp.concatenate([alpha, alpha], axis=1) + update
        acc_s[p, q_rows] = update


def _attn_diagonal(state, i, tq, bias_fn):
    half = tq // 2
    first_keys = pl.multiple_of(i * tq, tq)
    second_keys = pl.multiple_of(i * tq + half, half)
    _attn_block(*state, first_keys, half, lambda h: bias_fn(h, pl.ds(first_keys, half), 0, 0), first=True)
    _attn_block(*state, second_keys, half, lambda h: bias_fn(h, pl.ds(second_keys, half), half, half),
                q_rows=slice(half, tq))


def _attn_finish(o_ref, acc_s):
    for p in range(HEAD_PAIRS):
        acc = acc_s[p]
        o_ref[:, p * LANES:(p + 1) * LANES] = (acc[:, :LANES] / acc[:, LANES:]).astype(BF16)


def _attn_scratch(tq):
    return [pltpu.VMEM((N_HEADS, tq, LANES), BF16), pltpu.VMEM((N_HEADS, tq, LANES), F32),
            pltpu.VMEM((HEAD_PAIRS, tq, 2 * LANES), F32)]


def _fox_kernel(q_ref, k_ref, v_ref, ones_ref, c_ref, o_ref, qm_s, m_s, acc_s):
    tq = q_ref.shape[0]
    half = tq // 2
    i = pl.program_id(1)
    state = (qm_s, k_ref, v_ref, ones_ref, m_s, acc_s)
    _attn_begin(q_ref, qm_s)
    row = lax.broadcasted_iota(jnp.int32, (tq, half), 0)
    col = lax.broadcasted_iota(jnp.int32, (tq, half), 1)
    causal = jnp.where(row >= col, 0.0, NEG_INF)

    def gate_bias(h, key_rows):
        return c_ref[h:h + 1, key_rows] * (-LOG2E)

    _attn_diagonal(state, i, tq, lambda h, key_rows, q_lo, k_lo: (causal if q_lo == k_lo == 0 else causal[:half])
                   + gate_bias(h, key_rows))

    def full_block(j, _):
        start = pl.multiple_of(j * tq, tq)
        _attn_block(*state, start, tq, lambda h: gate_bias(h, pl.ds(start, tq)))
        return 0

    lax.fori_loop(0, i, full_block, 0)
    _attn_finish(o_ref, acc_s)


def _fox(q, k, v, c_rows, batch, seq):
    t = q.shape[0]
    tq = ATTN_TILE
    nq = seq // tq
    seq_block = pl.BlockSpec((None, seq, BRANCH_W), lambda b, i: (b, 0, 0))
    return pl.pallas_call(
        _fox_kernel,
        grid=(batch, nq),
        in_specs=[pl.BlockSpec((tq, BRANCH_W), lambda b, i: (b * nq + i, 0)), seq_block, seq_block,
                  _resident((2, tq, LANES)), pl.BlockSpec((None, N_HEADS, seq), lambda b, i: (b, 0, 0))],
        out_specs=pl.BlockSpec((tq, BRANCH_W), lambda b, i: (b * nq + i, 0)),
        out_shape=jax.ShapeDtypeStruct((t, BRANCH_W), BF16),
        scratch_shapes=_attn_scratch(tq),
        compiler_params=_params("arbitrary", "arbitrary"),
        name="fox_attention",
    )(q, k.reshape(batch, seq, BRANCH_W), v.reshape(batch, seq, BRANCH_W), _head_ones(tq), c_rows)


CLASS_LEN_WINDOW = DIL_MID_WINDOW // DIL_CLASSES
CLASS_FAR_STEP = DIL_FAR_STEP // DIL_CLASSES


def _dilated_kernel(q4_ref, k4_ref, v4_ref, q_ref, kp_ref, kc_ref, vp_ref, vc_ref, ones_ref, o_ref,
                    qm_s, m_s, acc_s, bias_s, acc_t, m_t):
    step = pl.program_id(1)
    tq = q_ref.shape[0]
    n = tq // DIL_CLASSES
    w = DIL_NEAR_WINDOW

    @pl.when((pl.program_id(0) == 0) & (step == 0))
    def _():
        row = lax.broadcasted_iota(jnp.int32, (tq, tq), 0)
        col = lax.broadcasted_iota(jnp.int32, (tq, tq), 1)
        e = row - col
        ok = e >= 0
        count = (ok & (e <= CLASS_LEN_WINDOW)).astype(F32) + (ok & (e % CLASS_FAR_STEP == 0)).astype(F32)
        bias_s[...] = jnp.log2(count)

    @pl.when(step < DIL_CLASSES)
    def _():
        state = (qm_s, k4_ref, v4_ref, ones_ref, m_s, acc_s)
        _attn_begin(q4_ref, qm_s)
        _attn_diagonal(state, 0, tq, lambda h, key_rows, q_lo, k_lo: bias_s[q_lo:, k_lo:k_lo + tq // 2])
        token_rows = pl.ds(step, n, stride=DIL_CLASSES)
        for tile in range(tq // n):
            rows = slice(tile * n, (tile + 1) * n)
            for h in range(N_HEADS):
                m_t[tile, h, token_rows, :] = m_s[h, rows]
            for p in range(HEAD_PAIRS):
                acc = acc_s[p, rows]
                acc_t[tile, p, 0, token_rows, :] = acc[:, :LANES]
                acc_t[tile, p, 1, token_rows, :] = acc[:, LANES:]

    @pl.when(step >= DIL_CLASSES)
    def _():
        i = step - DIL_CLASSES
        row = lax.broadcasted_iota(jnp.int32, (w, 2 * w), 0)
        col = lax.broadcasted_iota(jnp.int32, (w, 2 * w), 1)
        dist = row + w - col
        in_band = (dist >= 0) & (dist <= DIL_NEAR_WINDOW)
        bias = jnp.where(in_band, 0.0, NEG_INF)
        bias_first = jnp.where(in_band & ((col >= w) | (i > 0)), 0.0, NEG_INF)
        k_all = jnp.concatenate([kp_ref[...], kc_ref[...]], axis=0)
        v_all = jnp.concatenate([vp_ref[...], vc_ref[...]], axis=0)
        low_q = _low_head((w, LANES))
        for j in range(tq // w):
            rows = slice(j * w, (j + 1) * w)
            k = k_all[j * w:(j + 2) * w]
            for p in range(HEAD_PAIRS):
                rhs = _pair_values(v_all[j * w:(j + 2) * w, p * LANES:(p + 1) * LANES], ones_ref)
                kp = k[:, p * LANES:(p + 1) * LANES]
                probs, keep_class = [], []
                for e, qm in enumerate(_split_pair(q_ref[rows, p * LANES:(p + 1) * LANES])):
                    h = 2 * p + e
                    s = _mm_nt(qm, kp) + (bias_first if j == 0 else bias)
                    m_c = m_t[i, h, rows, :]
                    m = jnp.maximum(jnp.max(s, axis=-1, keepdims=True), m_c)
                    probs.append(jnp.exp2(s - jnp.concatenate([m, m], axis=1)).astype(BF16))
                    keep_class.append(m_c - m)
                a = jnp.exp2(jnp.where(low_q, keep_class[0], keep_class[1]))
                acc_class = jnp.concatenate([acc_t[i, p, 0, rows, :], acc_t[i, p, 1, rows, :]], axis=1)
                out = _mm(jnp.concatenate(probs, axis=1), rhs) + acc_class * jnp.concatenate([a, a], axis=1)
                o_ref[rows, p * LANES:(p + 1) * LANES] = (out[:, :LANES] / out[:, LANES:]).astype(BF16)


def _dilated(q, k, v, q4, k4, v4, batch, seq):
    t = q.shape[0]
    tq = ATTN_TILE
    nq = seq // tq
    sub = tq // DIL_NEAR_WINDOW
    tile_of = lambda s: jnp.maximum(s - DIL_CLASSES, 0)
    by_class = pl.BlockSpec((None, None, tq, BRANCH_W), lambda b, s: (b, jnp.minimum(s, DIL_CLASSES - 1), 0, 0))
    tile = pl.BlockSpec((tq, BRANCH_W), lambda b, s: (b * nq + tile_of(s), 0))
    prev = pl.BlockSpec((DIL_NEAR_WINDOW, BRANCH_W),
                        lambda b, s: ((b * nq + tile_of(s)) * sub - jnp.minimum(tile_of(s), 1), 0))
    return pl.pallas_call(
        _dilated_kernel,
        grid=(batch, DIL_CLASSES + nq),
        in_specs=[by_class, by_class, by_class, tile, prev, tile, prev, tile, _resident((2, tq, LANES))],
        out_specs=tile,
        out_shape=jax.ShapeDtypeStruct((t, BRANCH_W), BF16),
        scratch_shapes=_attn_scratch(tq) + [pltpu.VMEM((tq, tq), F32),
                                            pltpu.VMEM((nq, HEAD_PAIRS, 2, tq, LANES), F32),
                                            pltpu.VMEM((nq, N_HEADS, tq, LANES), F32)],
        compiler_params=_params("arbitrary", "arbitrary"),
        name="dilated_attention",
    )(q4, k4, v4, q, k, k, v, v, _head_ones(tq))


SWA_GROUP = N_HEADS // SWA_KV_HEADS
SWA_HEAD_ORDER = tuple(g * SWA_GROUP + p for p in range(SWA_GROUP) for g in range(SWA_KV_HEADS))


SWA_TILE = 512


def _swa_kernel(sink_ref, q_ref, kp_ref, kc_ref, vp_ref, vc_ref, ones_ref, o_ref):
    w = SWA_WINDOW
    i = pl.program_id(1)
    row = lax.broadcasted_iota(jnp.int32, (w, 2 * w), 0)
    col = lax.broadcasted_iota(jnp.int32, (w, 2 * w), 1)
    dist = row + w - col
    in_window = (dist >= 0) & (dist < SWA_WINDOW)
    bias = jnp.where(in_window, 0.0, NEG_INF)
    bias_first = jnp.where(in_window & ((col >= w) | (i > 0)), 0.0, NEG_INF)
    k_all = jnp.concatenate([kp_ref[...], kc_ref[...]], axis=0)
    v_all = jnp.concatenate([vp_ref[...], vc_ref[...]], axis=0)
    low_q = _low_head((w, LANES))
    for j in range(q_ref.shape[0] // w):
        k = k_all[j * w:(j + 2) * w]
        rhs = _pair_values(v_all[j * w:(j + 2) * w], ones_ref)
        for p in range(SWA_GROUP):
            probs, sink_drop = [], []
            for g, qm in enumerate(_split_pair(q_ref[j * w:(j + 1) * w, p * LANES:(p + 1) * LANES])):
                sink = sink_ref[g * SWA_GROUP + p] * LOG2E
                s = _mm_nt(qm, k) + (bias_first if j == 0 else bias)
                m = jnp.maximum(jnp.max(s, axis=-1, keepdims=True), sink)
                probs.append(jnp.exp2(s - m).astype(BF16))
                sink_drop.append(sink - m)
            out = _mm(jnp.concatenate(probs, axis=1), rhs)
            den = out[:, LANES:] + jnp.exp2(jnp.where(low_q, sink_drop[0], sink_drop[1]))
            o_ref[j * w:(j + 1) * w, p * LANES:(p + 1) * LANES] = (out[:, :LANES] / den).astype(BF16)


def _swa(q, k, v, sink, batch, seq):
    t = q.shape[0]
    tq = SWA_TILE
    nq = seq // tq
    sub = tq // SWA_WINDOW
    cur = pl.BlockSpec((tq, SWA_KW), lambda b, i: (b * nq + i, 0))
    prev = pl.BlockSpec((SWA_WINDOW, SWA_KW), lambda b, i: ((b * nq + i) * sub - jnp.minimum(i, 1), 0))
    return pl.pallas_call(
        _swa_kernel,
        grid=(batch, nq),
        in_specs=[pl.BlockSpec(memory_space=pltpu.SMEM),
                  pl.BlockSpec((tq, BRANCH_W), lambda b, i: (b * nq + i, 0)), prev, cur, prev, cur,
                  _resident((2, 2 * SWA_WINDOW, LANES))],
        out_specs=pl.BlockSpec((tq, BRANCH_W), lambda b, i: (b * nq + i, 0)),
        out_shape=jax.ShapeDtypeStruct((t, BRANCH_W), BF16),
        compiler_params=_params("parallel", "arbitrary"),
        name="swa_attention",
    )(sink, q, k, k, v, v, _head_ones(2 * SWA_WINDOW))


def _memkv_kernel(x_ref, g_ref, w_ref, o_ref):
    xn = _rmsnorm(x_ref[...], g_ref[...]).astype(BF16)
    o_ref[...] = _mm(xn, w_ref[...]).astype(BF16)


def _memkv(mem, g, w):
    t = mem.shape[0]
    return pl.pallas_call(
        _memkv_kernel,
        grid=(t // ROW_TILE,),
        in_specs=[_rows(ROW_TILE, D_MODEL), _resident(g.shape), _resident(w.shape)],
        out_specs=_rows(ROW_TILE, 2 * D_MODEL),
        out_shape=jax.ShapeDtypeStruct((t, 2 * D_MODEL), BF16),
        compiler_params=_params("parallel"),
        name="memory_kv",
    )(mem, g, w)


FF_CHUNK = 512
TAIL_VMEM_LIMIT = 58 * 1024 * 1024


def _tail_kernel(h_ref, ya_ref, yb_ref, wa_ref, wb_ref, wq_ref, kv_ref, wo_ref, wup_ref, wdown_ref, gains_ref,
                 o_ref, *, final_norm):
    gx, gm, gout = gains_ref[0:1, :], gains_ref[1:2, :], gains_ref[2:3, :]
    subs = [slice(r * SUB_TILE, (r + 1) * SUB_TILE) for r in range(h_ref.shape[0] // SUB_TILE)]
    h = [h_ref[rows, :] + _mm(ya_ref[rows, :], wa_ref[...]) + _mm(yb_ref[rows, :], wb_ref[...]) for rows in subs]
    xn = [_rmsnorm(x, gx).astype(BF16) for x in h]
    q = [(_mm(x, wq_ref[...]) * XA_SCALE).astype(BF16) for x in xn]
    outs = [[] for _ in subs]
    for hd in range(XA_HEADS):
        lo = hd * XA_HEAD_DIM
        s = [_mm_nt(x[:, lo:lo + XA_HEAD_DIM], kv_ref[:, lo:lo + XA_HEAD_DIM]) for x in q]
        p = [jnp.exp(x - jnp.max(x, axis=-1, keepdims=True)) for x in s]
        l = [jnp.sum(x, axis=-1, keepdims=True) for x in p]
        v = kv_ref[:, D_MODEL + lo:D_MODEL + lo + XA_HEAD_DIM]
        for r in range(len(subs)):
            outs[r].append((_mm(p[r].astype(BF16), v) / l[r]).astype(BF16))
    for r, rows in enumerate(subs):
        acc = h[r] + _mm(jnp.concatenate(outs[r], axis=1), wo_ref[...])
        xm = _rmsnorm(acc, gm).astype(BF16)
        for c in range(D_FF // FF_CHUNK):
            cols = slice(c * FF_CHUNK, (c + 1) * FF_CHUNK)
            up = jnp.maximum(_mm(xm, wup_ref[:, cols]), 0.0)
            acc = acc + _mm((up * up).astype(BF16), wdown_ref[cols, :])
        if final_norm:
            acc = _rmsnorm(acc, gout)
        o_ref[rows, :] = acc


def _tail(h, ya, yb, wa, wb, wq, kv, wo, wup, wdown, gains, seq, final_norm):
    t = h.shape[0]
    tiles_per_seq = seq // MLP_TILE
    return pl.pallas_call(
        functools.partial(_tail_kernel, final_norm=final_norm),
        grid=(t // MLP_TILE,),
        in_specs=[_rows(MLP_TILE, D_MODEL), _rows(MLP_TILE, BRANCH_W), _rows(MLP_TILE, BRANCH_W),
                  _resident(wa.shape), _resident(wb.shape), _resident(wq.shape),
                  pl.BlockSpec((None, MEM_LEN, 2 * D_MODEL), lambda i: (i // tiles_per_seq, 0, 0)),
                  _resident(wo.shape), _resident(wup.shape), _resident(wdown.shape), _resident(gains.shape)],
        out_specs=_rows(MLP_TILE, D_MODEL),
        out_shape=jax.ShapeDtypeStruct((t, D_MODEL), F32),
        compiler_params=pltpu.CompilerParams(dimension_semantics=("parallel",), vmem_limit_bytes=TAIL_VMEM_LIMIT),
        name="xattn_mlp",
    )(h, ya, yb, wa, wb, wq, kv, wo, wup, wdown, gains)


def _block_diag(w):
    g, c, d = w.shape
    eye = jnp.eye(g, dtype=w.dtype)
    return (w[:, :, None, :] * eye[:, None, :, None]).reshape(g * c, g * d)


def _rope_tables(seq):
    half = HEAD_DIM // 2
    inv = ROPE_THETA ** (-jnp.arange(half, dtype=F32) / half)
    ang = jnp.arange(seq, dtype=F32)[:, None] * inv[None, :]
    reps = LANES // half
    cos = jnp.tile(jnp.cos(ang), (1, reps))
    sign = jnp.tile(jnp.concatenate([-jnp.ones((half,), F32), jnp.ones((half,), F32)]), LANES // HEAD_DIM)
    sin = jnp.tile(jnp.sin(ang), (1, reps)) * sign[None, :]
    return cos, sin


def _permute_heads(w, axis):
    shape = w.shape
    split = shape[:axis] + (N_HEADS, HEAD_DIM) + shape[axis + 1:]
    return jnp.take(w.reshape(split), jnp.array(SWA_HEAD_ORDER), axis=axis).reshape(shape)


def _row(v, width=None):
    v = v.astype(F32).reshape(1, -1)
    if width is not None and v.shape[1] < width:
        v = jnp.pad(v, ((0, 0), (0, width - v.shape[1])))
    return v


def kernel(x, mem, ab_norm, ab_w_in, ab_conv_w, ab_conv_b, lru_w_a, lru_b_a, lru_w_i, lru_b_i, lru_lambda, fox_b_f, ab_w_out, cd_norm, cd_w_in, cd_sink, cd_w_out, xa_norm, xa_mem_norm, xa_w_q, xa_w_kv, xa_w_o, mlp_norm, mlp_w_up, mlp_w_down, final_norm):
    batch, seq, d = x.shape
    assert d == D_MODEL and seq % ROW_TILE == 0 and seq % ATTN_TILE == 0 and seq // DIL_CLASSES == ATTN_TILE and seq % SWA_TILE == 0 and mem.shape[1] == MEM_LEN
    depth = xa_norm.shape[0]
    h = x.reshape(batch * seq, d)
    mem_rows = mem.reshape(batch * MEM_LEN, d)
    cos, sin = _rope_tables(seq)
    main_w = 5 * BRANCH_W

    for layer in range(depth):
        j = layer // 2
        if layer % 2 == 0:
            w_in = ab_w_in[j]
            w_f = jnp.pad(w_in[:, main_w:], ((0, 0), (0, LANES - N_HEADS))).astype(BF16)
            ya, q, k, v, c = _inproj_ab(h, _row(ab_norm[j]), w_in[:, :main_w].astype(BF16), w_f,
                                        ab_conv_w[j].astype(F32), _row(ab_conv_b[j]),
                                        _block_diag(lru_w_a[j]).astype(BF16), _row(lru_b_a[j]),
                                        _block_diag(lru_w_i[j]).astype(BF16), _row(lru_b_i[j]),
                                        _row(lru_lambda[j]), _row(fox_b_f[j], LANES), seq)
            c_rows = c.reshape(batch, seq, LANES)[:, :, :N_HEADS].transpose(0, 2, 1)
            yb = _fox(q, k, v, c_rows, batch, seq)
            w_out = ab_w_out[j].astype(BF16)
        else:
            w_in = cd_w_in[j]
            qd_lo, qd_hi = 3 * BRANCH_W, 4 * BRANCH_W
            w_in = jnp.concatenate([w_in[:, :qd_lo], _permute_heads(w_in[:, qd_lo:qd_hi], 1), w_in[:, qd_hi:]], axis=1)
            qc, kc, vc, qd, kd, vd, qc4, kc4, vc4 = _inproj_cd(h, _row(cd_norm[j]), w_in.astype(BF16), cos, sin, batch, seq)
            by_class = (batch, DIL_CLASSES, seq // DIL_CLASSES, BRANCH_W)
            ya = _dilated(qc, kc, vc, qc4.reshape(by_class), kc4.reshape(by_class), vc4.reshape(by_class), batch, seq)
            yb = _swa(qd, kd, vd, cd_sink[j].astype(F32), batch, seq)
            w_out = cd_w_out[j]
            w_out = jnp.concatenate([w_out[:BRANCH_W], _permute_heads(w_out[BRANCH_W:], 0)], axis=0).astype(BF16)
        kv = _memkv(mem_rows, _row(xa_mem_norm[layer]), xa_w_kv[layer].astype(BF16))
        gains = jnp.concatenate([_row(xa_norm[layer]), _row(mlp_norm[layer]), _row(final_norm),
                                 jnp.zeros((SUBLANES - 3, D_MODEL), F32)], axis=0)
        h = _tail(h, ya, yb, w_out[:BRANCH_W], w_out[BRANCH_W:], xa_w_q[layer].astype(BF16),
                  kv.reshape(batch, MEM_LEN, 2 * D_MODEL), xa_w_o[layer].astype(BF16),
                  mlp_w_up[layer].astype(BF16), mlp_w_down[layer].astype(BF16), gains, seq,
                  final_norm=(layer == depth - 1))
    return h.reshape(batch, seq, d)
```

```python
import functools
import math

import jax
import jax.numpy as jnp
from jax import lax
from jax.experimental import pallas as pl
from jax.experimental.pallas import tpu as pltpu

F32 = jnp.float32
BF16 = jnp.bfloat16

D_MODEL = 1024
HEAD_DIM = 64
N_HEADS = 8
BRANCH_W = N_HEADS * HEAD_DIM
LRU_WIDTH = 512
LRU_BLOCKS = 8
CONV_WIDTH = 4
LRU_C = 8.0
SWA_KV_HEADS = 2
SWA_KW = SWA_KV_HEADS * HEAD_DIM
SWA_WINDOW = 128
DIL_NEAR_WINDOW = 128
DIL_MID_WINDOW = 512
DIL_MID_STEP = 4
DIL_FAR_STEP = 16
MEM_LEN = 256
XA_HEADS = 4
XA_HEAD_DIM = D_MODEL // XA_HEADS
D_FF = 4 * D_MODEL
ROPE_THETA = 10000.0
EPS = 1e-6
LOG2E = math.log2(math.e)
ATTN_SCALE = HEAD_DIM ** -0.5
Q_SCALE_LOG2 = ATTN_SCALE * LOG2E
XA_SCALE = XA_HEAD_DIM ** -0.5
LANES = 128
NEG_INF = float("-inf")

ROW_TILE = 512
SUB_TILE = 512
MLP_TILE = 1024
ATTN_TILE = 512
VMEM_LIMIT = 48 * 1024 * 1024


def _params(*sem):
    return pltpu.CompilerParams(dimension_semantics=sem, vmem_limit_bytes=VMEM_LIMIT)


def _resident(shape):
    zeros = (0,) * len(shape)
    return pl.BlockSpec(shape, lambda *_: zeros, pipeline_mode=pl.Buffered(1))


def _rows(tile, width):
    return pl.BlockSpec((tile, width), lambda i: (i, 0))


def _rmsnorm(x, g):
    return x * lax.rsqrt(jnp.mean(x * x, axis=-1, keepdims=True) + EPS) * g


def _mm(a, b):
    return jnp.dot(a, b, preferred_element_type=F32)


def _mm_nt(a, b):
    return lax.dot_general(a, b, (((1,), (1,)), ((), ())), preferred_element_type=F32)


DIL_CLASSES = 4
CLASS_ROWS = SUB_TILE // DIL_CLASSES


def _inproj_cd_kernel(x_ref, g_ref, w_ref, cos_ref, sin_ref, qc_ref, kc_ref, vc_ref, qd_ref, kd_ref, vd_ref,
                      qc4_ref, kc4_ref, vc4_ref, z_s):
    subs = [slice(r * SUB_TILE, (r + 1) * SUB_TILE) for r in range(x_ref.shape[0] // SUB_TILE)]
    xn = [_rmsnorm(x_ref[rows, :], g_ref[...]).astype(BF16) for rows in subs]
    lane = lax.broadcasted_iota(jnp.int32, (SUB_TILE, LANES), 1)
    first_half = (lane % HEAD_DIM) < (HEAD_DIM // 2)

    def rope(z, rows, scale):
        cos, sin = cos_ref[rows, :], sin_ref[rows, :]
        outs = []
        for c in range(z.shape[1] // LANES):
            zc = z[:, c * LANES:(c + 1) * LANES]
            partner = jnp.where(first_half, pltpu.roll(zc, LANES - HEAD_DIM // 2, 1), pltpu.roll(zc, HEAD_DIM // 2, 1))
            outs.append((zc * cos + partner * sin) * scale)
        return outs

    def lane_tiles(z):
        return [z[:, c * LANES:(c + 1) * LANES] for c in range(z.shape[1] // LANES)]

    w = BRANCH_W
    groups = [(qc_ref, qc4_ref, 0, w, Q_SCALE_LOG2), (kc_ref, kc4_ref, w, 2 * w, 1.0), (vc_ref, vc4_ref, 2 * w, 3 * w, None),
              (qd_ref, None, 3 * w, 4 * w, Q_SCALE_LOG2), (kd_ref, None, 4 * w, 4 * w + SWA_KW, 1.0),
              (vd_ref, None, 4 * w + SWA_KW, 4 * w + 2 * SWA_KW, None)]
    for out_ref, class_ref, lo, hi, scale in groups:
        z = [_mm(x, w_ref[:, lo:hi]) for x in xn]
        for r, (rows, zr) in enumerate(zip(subs, z)):
            tiles = lane_tiles(zr) if scale is None else rope(zr, rows, scale)
            out_ref[rows, :] = (tiles[0] if len(tiles) == 1 else jnp.concatenate(tiles, axis=1)).astype(BF16)
            if class_ref is not None:
                for c, t in enumerate(tiles):
                    z_s[r, c] = t
                for cls in range(DIL_CLASSES):
                    picked = [z_s[r, c, pl.ds(cls, CLASS_ROWS, stride=DIL_CLASSES), :] for c in range(len(tiles))]
                    class_ref[cls, r] = jnp.concatenate(picked, axis=1).astype(BF16)


def _inproj_cd(h, g, w, cos, sin, batch, seq):
    t = h.shape[0]
    sds = jax.ShapeDtypeStruct
    tiles_per_seq = seq // MLP_TILE
    subs = MLP_TILE // SUB_TILE
    table = pl.BlockSpec((MLP_TILE, LANES), lambda i: (i % tiles_per_seq, 0))
    wide, narrow = _rows(MLP_TILE, BRANCH_W), _rows(MLP_TILE, SWA_KW)
    by_class = pl.BlockSpec((None, DIL_CLASSES, subs, CLASS_ROWS, BRANCH_W),
                            lambda i: (i // tiles_per_seq, 0, i % tiles_per_seq, 0, 0))
    class_shape = sds((batch, DIL_CLASSES, seq // SUB_TILE, CLASS_ROWS, BRANCH_W), BF16)
    return pl.pallas_call(
        _inproj_cd_kernel,
        grid=(t // MLP_TILE,),
        in_specs=[_rows(MLP_TILE, D_MODEL), _resident(g.shape), _resident(w.shape), table, table],
        out_specs=[wide] * 4 + [narrow] * 2 + [by_class] * 3,
        out_shape=[sds((t, BRANCH_W), BF16)] * 4 + [sds((t, SWA_KW), BF16)] * 2 + [class_shape] * 3,
        scratch_shapes=[pltpu.VMEM((subs, BRANCH_W // LANES, SUB_TILE, LANES), F32)],
        compiler_params=_params("parallel"),
        name="inproj_cd",
    )(h, g, w, cos, sin)


SUBLANES = 8


def _group_rows(x):
    return x.reshape(x.shape[0] // SUBLANES, SUBLANES, x.shape[1])


def _scan_linear(a, x, h0):
    a3, x3 = _group_rows(a), _group_rows(x)
    row = lax.broadcasted_iota(jnp.int32, a3.shape, 1)
    d = 1
    while d < SUBLANES:
        keep = row >= d
        x3 = jnp.where(keep, x3 + a3 * pltpu.roll(x3, d, 1), x3)
        a3 = jnp.where(keep, a3 * pltpu.roll(a3, d, 1), a3)
        d *= 2
    out, carry = [], h0
    for g in range(a3.shape[0]):
        hg = a3[g] * carry + x3[g]
        out.append(hg)
        carry = hg[SUBLANES - 1:SUBLANES, :]
    return jnp.concatenate(out, axis=0)


def _scan_sum(x, c0):
    x3 = _group_rows(x)
    row = lax.broadcasted_iota(jnp.int32, x3.shape, 1)
    d = 1
    while d < SUBLANES:
        x3 = jnp.where(row >= d, x3 + pltpu.roll(x3, d, 1), x3)
        d *= 2
    out, carry = [], c0
    for g in range(x3.shape[0]):
        cg = x3[g] + carry
        out.append(cg)
        carry = cg[SUBLANES - 1:SUBLANES, :]
    return jnp.concatenate(out, axis=0)


def _softplus(x):
    return jnp.maximum(x, 0.0) + jnp.log1p(jnp.exp(-jnp.abs(x)))


def _sigmoid(x):
    return 1.0 / (1.0 + jnp.exp(-x))


def _gelu_tanh(x):
    return 0.5 * x * (1.0 + jnp.tanh(math.sqrt(2.0 / math.pi) * (x + 0.044715 * (x * x * x))))


CONV_TAIL = CONV_WIDTH * 2


def _reset_recurrence(ubuf, hprev, cprev):
    ubuf[0:CONV_TAIL, :] = jnp.zeros((CONV_TAIL, LRU_WIDTH), F32)
    hprev[...] = jnp.zeros_like(hprev)
    cprev[...] = jnp.zeros_like(cprev)


def _recurrent_branch(u, gate, f, cw_ref, cb_ref, wa_ref, ba_ref, wi_ref, bi_ref, lam_ref, bf_ref,
                      ubuf, hprev, cprev):
    tc = u.shape[0]
    tail = CONV_TAIL
    ubuf[tail:tail + tc, :] = u
    cw = cw_ref[...]
    conv = cb_ref[...] + cw[CONV_WIDTH - 1:CONV_WIDTH, :] * u
    for back in range(1, CONV_WIDTH):
        k = CONV_WIDTH - 1 - back
        conv = conv + cw[k:k + 1, :] * ubuf[tail - back:tail - back + tc, :]
    ubuf[0:tail, :] = u[tc - tail:tc, :]

    conv_b = conv.astype(BF16)
    r = _sigmoid(_mm(conv_b, wa_ref[...]) + ba_ref[...])
    gate_i = _sigmoid(_mm(conv_b, wi_ref[...]) + bi_ref[...])
    log_a = (-LRU_C) * r * _softplus(-lam_ref[...])
    a = jnp.exp(log_a)
    x_in = jnp.sqrt(-jnp.tanh(log_a) * (a * a + 1.0)) * (gate_i * conv)
    h = _scan_linear(a, x_in, hprev[SUBLANES - 1:SUBLANES, :])
    hprev[...] = h[tc - SUBLANES:tc, :]
    ya = h * _gelu_tanh(gate)

    z = f + bf_ref[...]
    log_f = jnp.minimum(z, 0.0) - jnp.log1p(jnp.exp(-jnp.abs(z)))
    c = _scan_sum(log_f, cprev[SUBLANES - 1:SUBLANES, :])
    cprev[...] = c[tc - SUBLANES:tc, :]
    return ya, c


def _inproj_ab_kernel(x_ref, g_ref, w_ref, wf_ref, cw_ref, cb_ref, wa_ref, ba_ref, wi_ref, bi_ref, lam_ref, bf_ref,
                      ya_ref, q_ref, k_ref, v_ref, c_ref, ubuf, hprev, cprev, *, tiles_per_seq):
    pl.when(pl.program_id(0) % tiles_per_seq == 0)(lambda: _reset_recurrence(ubuf, hprev, cprev))
    xn = _rmsnorm(x_ref[...], g_ref[...]).astype(BF16)
    w = BRANCH_W
    u = _mm(xn, w_ref[:, 0:w])
    gate = _mm(xn, w_ref[:, w:2 * w])
    f = _mm(xn, wf_ref[...])
    q_ref[...] = (_mm(xn, w_ref[:, 2 * w:3 * w]) * Q_SCALE_LOG2).astype(BF16)
    k_ref[...] = _mm(xn, w_ref[:, 3 * w:4 * w]).astype(BF16)
    v_ref[...] = _mm(xn, w_ref[:, 4 * w:5 * w]).astype(BF16)
    ya, c = _recurrent_branch(u, gate, f, cw_ref, cb_ref, wa_ref, ba_ref, wi_ref, bi_ref, lam_ref, bf_ref,
                              ubuf, hprev, cprev)
    ya_ref[...] = ya.astype(BF16)
    c_ref[...] = c


def _inproj_ab(h, g, w_main, w_f, conv_w, conv_b, wa, ba, wi, bi, lam, bf, seq):
    t = h.shape[0]
    sds = jax.ShapeDtypeStruct
    small = [g, w_main, w_f, conv_w, conv_b, wa, ba, wi, bi, lam, bf]
    return pl.pallas_call(
        functools.partial(_inproj_ab_kernel, tiles_per_seq=seq // ROW_TILE),
        grid=(t // ROW_TILE,),
        in_specs=[_rows(ROW_TILE, D_MODEL)] + [_resident(a.shape) for a in small],
        out_specs=[_rows(ROW_TILE, BRANCH_W)] * 4 + [_rows(ROW_TILE, LANES)],
        out_shape=[sds((t, BRANCH_W), BF16)] * 4 + [sds((t, LANES), F32)],
        scratch_shapes=[pltpu.VMEM((ROW_TILE + CONV_TAIL, LRU_WIDTH), F32),
                        pltpu.VMEM((SUBLANES, LRU_WIDTH), F32), pltpu.VMEM((SUBLANES, LANES), F32)],
        compiler_params=_params("arbitrary"),
        name="inproj_ab_rglru",
    )(h, *small)


HEAD_PAIRS = N_HEADS // 2


def _pair(ref, rows, p):
    return ref[rows, p * LANES:(p + 1) * LANES]


def _low_head(shape):
    return lax.broadcasted_iota(jnp.int32, shape, 1) < HEAD_DIM


def _split_pair(x):
    low = _low_head(x.shape)
    zero = jnp.zeros_like(x)
    return jnp.where(low, x, zero), jnp.where(low, zero, x)


def _pair_values(v, ones_ref):
    keys = v.shape[0]
    v_lo, v_hi = _split_pair(v)
    return jnp.concatenate([jnp.concatenate([v_lo, ones_ref[0, :keys]], axis=1),
                            jnp.concatenate([v_hi, ones_ref[1, :keys]], axis=1)], axis=0)


def _head_ones(keys):
    low = (jnp.arange(LANES) < HEAD_DIM).astype(BF16)
    return jnp.broadcast_to(jnp.stack([low, 1 - low])[:, None, :], (2, keys, LANES))


def _attn_begin(q_ref, qm_s):
    for p in range(HEAD_PAIRS):
        qm_s[2 * p], qm_s[2 * p + 1] = _split_pair(_pair(q_ref, slice(None), p))


def _attn_block(qm_s, k_ref, v_ref, ones_ref, m_s, acc_s, start, width, bias_fn, q_rows=slice(None), first=False):
    rows = pl.ds(start, width)
    nq = len(range(*q_rows.indices(qm_s.shape[1])))
    low_q = _low_head((nq, LANES))
    for p in range(HEAD_PAIRS):
        kp = _pair(k_ref, rows, p)
        rhs = _pair_values(_pair(v_ref, rows, p), ones_ref)
        probs, drops = [], []
        for e in range(2):
            h = 2 * p + e
            s = _mm_nt(qm_s[h, q_rows], kp) + bias_fn(h)
            m_new = jnp.max(s, axis=-1, keepdims=True)
            if first:
                m_new = jnp.broadcast_to(m_new, (nq, LANES))
            else:
                m_prev = m_s[h, q_rows]
                m_new = jnp.maximum(m_prev, m_new)
                drops.append(m_prev - m_new)
            m_s[h, q_rows] = m_new
            probs.append(jnp.exp2(s - jnp.concatenate([m_new] * (width // LANES), axis=1)).astype(BF16))
        update = _mm(jnp.concatenate(probs, axis=1), rhs)
        if not first:
            alpha = jnp.exp2(jnp.where(low_q, drops[0], drops[1]))
            update = acc_s[p, q_rows] * jnp.concatenate([alpha, alpha], axis=1) + update
        acc_s[p, q_rows] = update


def _attn_diagonal(state, i, tq, bias_fn):
    half = tq // 2
    first_keys = pl.multiple_of(i * tq, tq)
    second_keys = pl.multiple_of(i * tq + half, half)
    _attn_block(*state, first_keys, half, lambda h: bias_fn(h, pl.ds(first_keys, half), 0, 0), first=True)
    _attn_block(*state, second_keys, half, lambda h: bias_fn(h, pl.ds(second_keys, half), half, half),
                q_rows=slice(half, tq))


def _attn_finish(o_ref, acc_s):
    for p in range(HEAD_PAIRS):
        acc = acc_s[p]
        o_ref[:, p * LANES:(p + 1) * LANES] = (acc[:, :LANES] / acc[:, LANES:]).astype(BF16)


def _attn_scratch(tq):
    return [pltpu.VMEM((N_HEADS, tq, LANES), BF16), pltpu.VMEM((N_HEADS, tq, LANES), F32),
            pltpu.VMEM((HEAD_PAIRS, tq, 2 * LANES), F32)]


def _fox_kernel(q_ref, k_ref, v_ref, ones_ref, c_ref, o_ref, qm_s, m_s, acc_s):
    tq = q_ref.shape[0]
    half = tq // 2
    i = pl.program_id(1)
    state = (qm_s, k_ref, v_ref, ones_ref, m_s, acc_s)
    _attn_begin(q_ref, qm_s)
    row = lax.broadcasted_iota(jnp.int32, (tq, half), 0)
    col = lax.broadcasted_iota(jnp.int32, (tq, half), 1)
    causal = jnp.where(row >= col, 0.0, NEG_INF)

    def gate_bias(h, key_rows):
        return c_ref[h:h + 1, key_rows] * (-LOG2E)

    _attn_diagonal(state, i, tq, lambda h, key_rows, q_lo, k_lo: (causal if q_lo == k_lo == 0 else causal[:half])
                   + gate_bias(h, key_rows))

    def full_block(j, _):
        start = pl.multiple_of(j * tq, tq)
        _attn_block(*state, start, tq, lambda h: gate_bias(h, pl.ds(start, tq)))
        return 0

    lax.fori_loop(0, i, full_block, 0)
    _attn_finish(o_ref, acc_s)


def _fox(q, k, v, c_rows, batch, seq):
    t = q.shape[0]
    tq = ATTN_TILE
    nq = seq // tq
    seq_block = pl.BlockSpec((None, seq, BRANCH_W), lambda b, i: (b, 0, 0))
    return pl.pallas_call(
        _fox_kernel,
        grid=(batch, nq),
        in_specs=[pl.BlockSpec((tq, BRANCH_W), lambda b, i: (b * nq + i, 0)), seq_block, seq_block,
                  _resident((2, tq, LANES)), pl.BlockSpec((None, N_HEADS, seq), lambda b, i: (b, 0, 0))],
        out_specs=pl.BlockSpec((tq, BRANCH_W), lambda b, i: (b * nq + i, 0)),
        out_shape=jax.ShapeDtypeStruct((t, BRANCH_W), BF16),
        scratch_shapes=_attn_scratch(tq),
        compiler_params=_params("arbitrary", "arbitrary"),
        name="fox_attention",
    )(q, k.reshape(batch, seq, BRANCH_W), v.reshape(batch, seq, BRANCH_W), _head_ones(tq), c_rows)


CLASS_LEN_WINDOW = DIL_MID_WINDOW // DIL_CLASSES
CLASS_FAR_STEP = DIL_FAR_STEP // DIL_CLASSES


def _dilated_kernel(q4_ref, k4_ref, v4_ref, q_ref, kp_ref, kc_ref, vp_ref, vc_ref, ones_ref, o_ref,
                    qm_s, m_s, acc_s, bias_s, acc_t, m_t):
    step = pl.program_id(1)
    tq = q_ref.shape[0]
    n = tq // DIL_CLASSES
    w = DIL_NEAR_WINDOW

    @pl.when((pl.program_id(0) == 0) & (step == 0))
    def _():
        row = lax.broadcasted_iota(jnp.int32, (tq, tq), 0)
        col = lax.broadcasted_iota(jnp.int32, (tq, tq), 1)
        e = row - col
        ok = e >= 0
        count = (ok & (e <= CLASS_LEN_WINDOW)).astype(F32) + (ok & (e % CLASS_FAR_STEP == 0)).astype(F32)
        bias_s[...] = jnp.log2(count)

    @pl.when(step < DIL_CLASSES)
    def _():
        state = (qm_s, k4_ref, v4_ref, ones_ref, m_s, acc_s)
        _attn_begin(q4_ref, qm_s)
        _attn_diagonal(state, 0, tq, lambda h, key_rows, q_lo, k_lo: bias_s[q_lo:, k_lo:k_lo + tq // 2])
        token_rows = pl.ds(step, n, stride=DIL_CLASSES)
        for tile in range(tq // n):
            rows = slice(tile * n, (tile + 1) * n)
            for h in range(N_HEADS):
                m_t[tile, h, token_rows, :] = m_s[h, rows]
            for p in range(HEAD_PAIRS):
                acc = acc_s[p, rows]
                acc_t[tile, p, 0, token_rows, :] = acc[:, :LANES]
                acc_t[tile, p, 1, token_rows, :] = acc[:, LANES:]

    @pl.when(step >= DIL_CLASSES)
    def _():
        i = step - DIL_CLASSES
        row = lax.broadcasted_iota(jnp.int32, (w, 2 * w), 0)
        col = lax.broadcasted_iota(jnp.int32, (w, 2 * w), 1)
        dist = row + w - col
        in_band = (dist >= 0) & (dist <= DIL_NEAR_WINDOW)
        bias = jnp.where(in_band, 0.0, NEG_INF)
        bias_first = jnp.where(in_band & ((col >= w) | (i > 0)), 0.0, NEG_INF)
        k_all = jnp.concatenate([kp_ref[...], kc_ref[...]], axis=0)
        v_all = jnp.concatenate([vp_ref[...], vc_ref[...]], axis=0)
        low_q = _low_head((w, LANES))
        for j in range(tq // w):
            rows = slice(j * w, (j + 1) * w)
            k = k_all[j * w:(j + 2) * w]
            for p in range(HEAD_PAIRS):
                rhs = _pair_values(v_all[j * w:(j + 2) * w, p * LANES:(p + 1) * LANES], ones_ref)
                kp = k[:, p * LANES:(p + 1) * LANES]
                probs, keep_class = [], []
                for e, qm in enumerate(_split_pair(q_ref[rows, p * LANES:(p + 1) * LANES])):
                    h = 2 * p + e
                    s = _mm_nt(qm, kp) + (bias_first if j == 0 else bias)
                    m_c = m_t[i, h, rows, :]
                    m = jnp.maximum(jnp.max(s, axis=-1, keepdims=True), m_c)
                    probs.append(jnp.exp2(s - jnp.concatenate([m, m], axis=1)).astype(BF16))
                    keep_class.append(m_c - m)
                a = jnp.exp2(jnp.where(low_q, keep_class[0], keep_class[1]))
                acc_class = jnp.concatenate([acc_t[i, p, 0, rows, :], acc_t[i, p, 1, rows, :]], axis=1)
                out = _mm(jnp.concatenate(probs, axis=1), rhs) + acc_class * jnp.concatenate([a, a], axis=1)
                o_ref[rows, p * LANES:(p + 1) * LANES] = (out[:, :LANES] / out[:, LANES:]).astype(BF16)


def _dilated(q, k, v, q4, k4, v4, batch, seq):
    t = q.shape[0]
    tq = ATTN_TILE
    nq = seq // tq
    sub = tq // DIL_NEAR_WINDOW
    tile_of = lambda s: jnp.maximum(s - DIL_CLASSES, 0)
    by_class = pl.BlockSpec((None, None, tq, BRANCH_W), lambda b, s: (b, jnp.minimum(s, DIL_CLASSES - 1), 0, 0))
    tile = pl.BlockSpec((tq, BRANCH_W), lambda b, s: (b * nq + tile_of(s), 0))
    prev = pl.BlockSpec((DIL_NEAR_WINDOW, BRANCH_W),
                        lambda b, s: ((b * nq + tile_of(s)) * sub - jnp.minimum(tile_of(s), 1), 0))
    return pl.pallas_call(
        _dilated_kernel,
        grid=(batch, DIL_CLASSES + nq),
        in_specs=[by_class, by_class, by_class, tile, prev, tile, prev, tile, _resident((2, tq, LANES))],
        out_specs=tile,
        out_shape=jax.ShapeDtypeStruct((t, BRANCH_W), BF16),
        scratch_shapes=_attn_scratch(tq) + [pltpu.VMEM((tq, tq), F32),
                                            pltpu.VMEM((nq, HEAD_PAIRS, 2, tq, LANES), F32),
                                            pltpu.VMEM((nq, N_HEADS, tq, LANES), F32)],
        compiler_params=_params("arbitrary", "arbitrary"),
        name="dilated_attention",
    )(q4, k4, v4, q, k, k, v, v, _head_ones(tq))


SWA_GROUP = N_HEADS // SWA_KV_HEADS
SWA_HEAD_ORDER = tuple(g * SWA_GROUP + p for p in range(SWA_GROUP) for g in range(SWA_KV_HEADS))


SWA_TILE = 512


def _swa_kernel(sink_ref, q_ref, kp_ref, kc_ref, vp_ref, vc_ref, ones_ref, o_ref):
    w = SWA_WINDOW
    i = pl.program_id(1)
    row = lax.broadcasted_iota(jnp.int32, (w, 2 * w), 0)
    col = lax.broadcasted_iota(jnp.int32, (w, 2 * w), 1)
    dist = row + w - col
    in_window = (dist >= 0) & (dist < SWA_WINDOW)
    bias = jnp.where(in_window, 0.0, NEG_INF)
    bias_first = jnp.where(in_window & ((col >= w) | (i > 0)), 0.0, NEG_INF)
    k_all = jnp.concatenate([kp_ref[...], kc_ref[...]], axis=0)
    v_all = jnp.concatenate([vp_ref[...], vc_ref[...]], axis=0)
    low_q = _low_head((w, LANES))
    for j in range(q_ref.shape[0] // w):
        k = k_all[j * w:(j + 2) * w]
        rhs = _pair_values(v_all[j * w:(j + 2) * w], ones_ref)
        for p in range(SWA_GROUP):
            probs, sink_drop = [], []
            for g, qm in enumerate(_split_pair(q_ref[j * w:(j + 1) * w, p * LANES:(p + 1) * LANES])):
                sink = sink_ref[g * SWA_GROUP + p] * LOG2E
                s = _mm_nt(qm, k) + (bias_first if j == 0 else bias)
                m = jnp.maximum(jnp.max(s, axis=-1, keepdims=True), sink)
                probs.append(jnp.exp2(s - m).astype(BF16))
                sink_drop.append(sink - m)
            out = _mm(jnp.concatenate(probs, axis=1), rhs)
            den = out[:, LANES:] + jnp.exp2(jnp.where(low_q, sink_drop[0], sink_drop[1]))
            o_ref[j * w:(j + 1) * w, p * LANES:(p + 1) * LANES] = (out[:, :LANES] / den).astype(BF16)


def _swa(q, k, v, sink, batch, seq):
    t = q.shape[0]
    tq = SWA_TILE
    nq = seq // tq
    sub = tq // SWA_WINDOW
    cur = pl.BlockSpec((tq, SWA_KW), lambda b, i: (b * nq + i, 0))
    prev = pl.BlockSpec((SWA_WINDOW, SWA_KW), lambda b, i: ((b * nq + i) * sub - jnp.minimum(i, 1), 0))
    return pl.pallas_call(
        _swa_kernel,
        grid=(batch, nq),
        in_specs=[pl.BlockSpec(memory_space=pltpu.SMEM),
                  pl.BlockSpec((tq, BRANCH_W), lambda b, i: (b * nq + i, 0)), prev, cur, prev, cur,
                  _resident((2, 2 * SWA_WINDOW, LANES))],
        out_specs=pl.BlockSpec((tq, BRANCH_W), lambda b, i: (b * nq + i, 0)),
        out_shape=jax.ShapeDtypeStruct((t, BRANCH_W), BF16),
        compiler_params=_params("parallel", "arbitrary"),
        name="swa_attention",
    )(sink, q, k, k, v, v, _head_ones(2 * SWA_WINDOW))


def _memkv_kernel(x_ref, g_ref, w_ref, o_ref):
    xn = _rmsnorm(x_ref[...], g_ref[...]).astype(BF16)
    o_ref[...] = _mm(xn, w_ref[...]).astype(BF16)


def _memkv(mem, g, w):
    t = mem.shape[0]
    return pl.pallas_call(
        _memkv_kernel,
        grid=(t // ROW_TILE,),
        in_specs=[_rows(ROW_TILE, D_MODEL), _resident(g.shape), _resident(w.shape)],
        out_specs=_rows(ROW_TILE, 2 * D_MODEL),
        out_shape=jax.ShapeDtypeStruct((t, 2 * D_MODEL), BF16),
        compiler_params=_params("parallel"),
        name="memory_kv",
    )(mem, g, w)


def _xattn_kernel(h_ref, ya_ref, yb_ref, wa_ref, wb_ref, g_ref, wq_ref, kv_ref, wo_ref, o_ref):
    subs = [slice(r * SUB_TILE, (r + 1) * SUB_TILE) for r in range(h_ref.shape[0] // SUB_TILE)]
    h = [h_ref[rows, :] + _mm(ya_ref[rows, :], wa_ref[...]) + _mm(yb_ref[rows, :], wb_ref[...]) for rows in subs]
    xn = [_rmsnorm(x, g_ref[...]).astype(BF16) for x in h]
    q = [(_mm(x, wq_ref[...]) * XA_SCALE).astype(BF16) for x in xn]
    outs = [[] for _ in subs]
    for hd in range(XA_HEADS):
        lo = hd * XA_HEAD_DIM
        s = [_mm_nt(x[:, lo:lo + XA_HEAD_DIM], kv_ref[:, lo:lo + XA_HEAD_DIM]) for x in q]
        p = [jnp.exp(x - jnp.max(x, axis=-1, keepdims=True)) for x in s]
        l = [jnp.sum(x, axis=-1, keepdims=True) for x in p]
        v = kv_ref[:, D_MODEL + lo:D_MODEL + lo + XA_HEAD_DIM]
        for r in range(len(subs)):
            outs[r].append((_mm(p[r].astype(BF16), v) / l[r]).astype(BF16))
    for r, rows in enumerate(subs):
        o_ref[rows, :] = h[r] + _mm(jnp.concatenate(outs[r], axis=1), wo_ref[...])


def _xattn(h, ya, yb, wa, wb, g, wq, kv, wo, seq):
    t = h.shape[0]
    tiles_per_seq = seq // MLP_TILE
    return pl.pallas_call(
        _xattn_kernel,
        grid=(t // MLP_TILE,),
        in_specs=[_rows(MLP_TILE, D_MODEL), _rows(MLP_TILE, BRANCH_W), _rows(MLP_TILE, BRANCH_W),
                  _resident(wa.shape), _resident(wb.shape), _resident(g.shape), _resident(wq.shape),
                  pl.BlockSpec((None, MEM_LEN, 2 * D_MODEL), lambda i: (i // tiles_per_seq, 0, 0)),
                  _resident(wo.shape)],
        out_specs=_rows(MLP_TILE, D_MODEL),
        out_shape=jax.ShapeDtypeStruct((t, D_MODEL), F32),
        compiler_params=_params("parallel"),
        name="memory_xattn",
    )(h, ya, yb, wa, wb, g, wq, kv, wo)


FF_CHUNK = 512


def _mlp_kernel(h_ref, g_ref, wup_ref, wdown_ref, gout_ref, o_ref, *, final_norm):
    for r in range(h_ref.shape[0] // SUB_TILE):
        rows = slice(r * SUB_TILE, (r + 1) * SUB_TILE)
        h = h_ref[rows, :]
        xn = _rmsnorm(h, g_ref[...]).astype(BF16)
        acc = h
        for c in range(D_FF // FF_CHUNK):
            cols = slice(c * FF_CHUNK, (c + 1) * FF_CHUNK)
            up = jnp.maximum(_mm(xn, wup_ref[:, cols]), 0.0)
            acc = acc + _mm((up * up).astype(BF16), wdown_ref[cols, :])
        if final_norm:
            acc = _rmsnorm(acc, gout_ref[...])
        o_ref[rows, :] = acc


def _mlp(h, g, wup, wdown, gout, final_norm):
    t = h.shape[0]
    return pl.pallas_call(
        functools.partial(_mlp_kernel, final_norm=final_norm),
        grid=(t // MLP_TILE,),
        in_specs=[_rows(MLP_TILE, D_MODEL), _resident(g.shape), _resident(wup.shape), _resident(wdown.shape),
                  _resident(gout.shape)],
        out_specs=_rows(MLP_TILE, D_MODEL),
        out_shape=jax.ShapeDtypeStruct((t, D_MODEL), F32),
        compiler_params=_params("parallel"),
        name="relu2_mlp",
    )(h, g, wup, wdown, gout)


def _block_diag(w):
    g, c, d = w.shape
    eye = jnp.eye(g, dtype=w.dtype)
    return (w[:, :, None, :] * eye[:, None, :, None]).reshape(g * c, g * d)


def _rope_tables(seq):
    half = HEAD_DIM // 2
    inv = ROPE_THETA ** (-jnp.arange(half, dtype=F32) / half)
    ang = jnp.arange(seq, dtype=F32)[:, None] * inv[None, :]
    reps = LANES // half
    cos = jnp.tile(jnp.cos(ang), (1, reps))
    sign = jnp.tile(jnp.concatenate([-jnp.ones((half,), F32), jnp.ones((half,), F32)]), LANES // HEAD_DIM)
    sin = jnp.tile(jnp.sin(ang), (1, reps)) * sign[None, :]
    return cos, sin


def _permute_heads(w, axis):
    shape = w.shape
    split = shape[:axis] + (N_HEADS, HEAD_DIM) + shape[axis + 1:]
    return jnp.take(w.reshape(split), jnp.array(SWA_HEAD_ORDER), axis=axis).reshape(shape)


def _row(v, width=None):
    v = v.astype(F32).reshape(1, -1)
    if width is not None and v.shape[1] < width:
        v = jnp.pad(v, ((0, 0), (0, width - v.shape[1])))
    return v


def kernel(x, mem, ab_norm, ab_w_in, ab_conv_w, ab_conv_b, lru_w_a, lru_b_a, lru_w_i, lru_b_i, lru_lambda, fox_b_f, ab_w_out, cd_norm, cd_w_in, cd_sink, cd_w_out, xa_norm, xa_mem_norm, xa_w_q, xa_w_kv, xa_w_o, mlp_norm, mlp_w_up, mlp_w_down, final_norm):
    batch, seq, d = x.shape
    assert d == D_MODEL and seq % ROW_TILE == 0 and seq % ATTN_TILE == 0 and seq // DIL_CLASSES == ATTN_TILE and seq % SWA_TILE == 0 and mem.shape[1] == MEM_LEN
    depth = xa_norm.shape[0]
    h = x.reshape(batch * seq, d)
    mem_rows = mem.reshape(batch * MEM_LEN, d)
    cos, sin = _rope_tables(seq)
    main_w = 5 * BRANCH_W

    for layer in range(depth):
        j = layer // 2
        if layer % 2 == 0:
            w_in = ab_w_in[j]
            w_f = jnp.pad(w_in[:, main_w:], ((0, 0), (0, LANES - N_HEADS))).astype(BF16)
            ya, q, k, v, c = _inproj_ab(h, _row(ab_norm[j]), w_in[:, :main_w].astype(BF16), w_f,
                                        ab_conv_w[j].astype(F32), _row(ab_conv_b[j]),
                                        _block_diag(lru_w_a[j]).astype(BF16), _row(lru_b_a[j]),
                                        _block_diag(lru_w_i[j]).astype(BF16), _row(lru_b_i[j]),
                                        _row(lru_lambda[j]), _row(fox_b_f[j], LANES), seq)
            c_rows = c.reshape(batch, seq, LANES)[:, :, :N_HEADS].transpose(0, 2, 1)
            yb = _fox(q, k, v, c_rows, batch, seq)
            w_out = ab_w_out[j].astype(BF16)
        else:
            w_in = cd_w_in[j]
            qd_lo, qd_hi = 3 * BRANCH_W, 4 * BRANCH_W
            w_in = jnp.concatenate([w_in[:, :qd_lo], _permute_heads(w_in[:, qd_lo:qd_hi], 1), w_in[:, qd_hi:]], axis=1)
            qc, kc, vc, qd, kd, vd, qc4, kc4, vc4 = _inproj_cd(h, _row(cd_norm[j]), w_in.astype(BF16), cos, sin, batch, seq)
            by_class = (batch, DIL_CLASSES, seq // DIL_CLASSES, BRANCH_W)
            ya = _dilated(qc, kc, vc, qc4.reshape(by_class), kc4.reshape(by_class), vc4.reshape(by_class), batch, seq)
            yb = _swa(qd, kd, vd, cd_sink[j].astype(F32), batch, seq)
            w_out = cd_w_out[j]
            w_out = jnp.concatenate([w_out[:BRANCH_W], _permute_heads(w_out[BRANCH_W:], 0)], axis=0).astype(BF16)
        kv = _memkv(mem_rows, _row(xa_mem_norm[layer]), xa_w_kv[layer].astype(BF16))
        h = _xattn(h, ya, yb, w_out[:BRANCH_W], w_out[BRANCH_W:], _row(xa_norm[layer]), xa_w_q[layer].astype(BF16),
                   kv.reshape(batch, MEM_LEN, 2 * D_MODEL), xa_w_o[layer].astype(BF16), seq)
        h = _mlp(h, _row(mlp_norm[layer]), mlp_w_up[layer].astype(BF16), mlp_w_down[layer].astype(BF16),
                 _row(final_norm), final_norm=(layer == depth - 1))
    return h.reshape(batch, seq, d)
```
